```python
import math
import jax, jax.numpy as jnp
from jax import lax
import numpy as np

D_MODEL = 1024
BATCH = 8
SEQ = 8192
DEPTH = 1
DEC_BATCH = 128
DEC_SEQ = 1
PAST_LEN = 8192
PAGE_SIZE = 128

HEAD_DIM = 64
FOX_HEADS = 8
FOX_WIDTH = FOX_HEADS * HEAD_DIM
DIFF_HEADS = 4
DIFF_QK = DIFF_HEADS * 2 * HEAD_DIM
DIFF_VDIM = 2 * HEAD_DIM
DIFF_WIDTH = DIFF_HEADS * DIFF_VDIM
MIX_WIDTH = FOX_WIDTH + DIFF_WIDTH
COL_SIZES = (FOX_WIDTH, FOX_WIDTH, FOX_WIDTH, FOX_HEADS, FOX_WIDTH, DIFF_QK, DIFF_QK, DIFF_WIDTH)
N_IN = sum(COL_SIZES)
MEM_TOKENS = 256
MEM_HEADS = 4
MEM_HEAD_DIM = D_MODEL // MEM_HEADS
D_FF = 2816
CONV_WIDTH = 3
Q_BLOCK = 128
ROPE_THETA = 10000.0
EPS = 1e-6

kernel_name = "hymba_fox_diffattn_memxattn_convffn_step"


def rmsnorm(x, g):
    xf = x.astype(jnp.float32)
    y = xf * lax.rsqrt(jnp.mean(xf * xf, axis=-1, keepdims=True) + EPS)
    return (y * g.astype(jnp.float32)).astype(x.dtype)


def rope(x, pos):
    d = x.shape[-1]
    half = d // 2
    inv = ROPE_THETA ** (-2.0 * jnp.arange(half, dtype=jnp.float32) / d)
    ang = pos.astype(jnp.float32)[:, None] * inv
    ang = ang.reshape((ang.shape[0],) + (1,) * (x.ndim - 3) + (half,))
    cos, sin = jnp.cos(ang), jnp.sin(ang)
    xf = x.astype(jnp.float32)
    x1, x2 = xf[..., :half], xf[..., half:]
    return jnp.concatenate([x1 * cos - x2 * sin, x2 * cos + x1 * sin], axis=-1).astype(x.dtype)


def mixer_proj(h, pos, w_in, b_f, fox_qg, fox_kg, diff_qg, diff_kg):
    B, T, _ = h.shape
    z = h @ w_in
    idx = [int(i) for i in np.cumsum(COL_SIZES)[:-1]]
    fq, fk, fv, ff, fg, dq, dk, dv = jnp.split(z, idx, axis=-1)
    fq = rmsnorm(fq.reshape(B, T, FOX_HEADS, HEAD_DIM), fox_qg)
    fk = rmsnorm(fk.reshape(B, T, FOX_HEADS, HEAD_DIM), fox_kg)
    fv = fv.reshape(B, T, FOX_HEADS, HEAD_DIM)
    logf = jax.nn.log_sigmoid((ff + b_f).astype(jnp.float32))
    dq = rope(rmsnorm(dq.reshape(B, T, DIFF_HEADS, 2, HEAD_DIM), diff_qg), pos)
    dk = rope(rmsnorm(dk.reshape(B, T, DIFF_HEADS, 2, HEAD_DIM), diff_kg), pos)
    dv = dv.reshape(B, T, DIFF_HEADS, DIFF_VDIM)
    return fq, fk, fv, logf, fg, dq, dk, dv


def fox_attend(q, cq, qpos, k, ck, v, kpos):
    s = jnp.einsum('bqhd,bkhd->bhqk', q, k).astype(jnp.float32) * (HEAD_DIM ** -0.5)
    s = s + jnp.transpose(cq, (0, 2, 1))[..., :, None] - jnp.transpose(ck, (0, 2, 1))[..., None, :]
    mask = kpos[None, :] <= qpos[:, None]
    p = jax.nn.softmax(jnp.where(mask, s, -jnp.inf), axis=-1)
    return jnp.einsum('bhqk,bkhd->bqhd', p.astype(v.dtype), v)


def diff_attend(q, qpos, k, v, kpos, lam):
    s = jnp.einsum('bqhmd,bkhmd->bhmqk', q, k).astype(jnp.float32) * (HEAD_DIM ** -0.5)
    mask = kpos[None, :] <= qpos[:, None]
    p = jax.nn.softmax(jnp.where(mask, s, -jnp.inf), axis=-1)
    a = p[:, :, 0] - lam * p[:, :, 1]
    return jnp.einsum('bhqk,bkhe->bqhe', a.astype(v.dtype), v)


def prompt_sweep(fq, fk, fv, c, dq, dk, dv, lam):
    B, T = fq.shape[:2]
    kpos = jnp.arange(T)

    def block(i):
        start = i * Q_BLOCK
        sl = lambda a: lax.dynamic_slice_in_dim(a, start, Q_BLOCK, axis=1)
        qpos = start + jnp.arange(Q_BLOCK)
        fo = fox_attend(sl(fq), sl(c), qpos, fk, c, fv, kpos)
        do = diff_attend(sl(dq), qpos, dk, dv, kpos, lam)
        return fo, do

    fo, do = lax.map(block, jnp.arange(T // Q_BLOCK))
    unblock = lambda a: jnp.moveaxis(a, 0, 1).reshape((B, T) + a.shape[3:])
    return unblock(fo), unblock(do)


def merge_groups(fo, fg, do, subln_g, lam_init, w_out):
    B, T = fo.shape[:2]
    fox = fo.reshape(B, T, FOX_WIDTH) * jax.nn.sigmoid(fg)
    diff = (rmsnorm(do, subln_g) * (1.0 - lam_init)).reshape(B, T, DIFF_WIDTH)
    return jnp.concatenate([fox, diff], axis=-1) @ w_out


def mem_kv(mem, mem_norm_g, w_ckv, kg):
    B, M, _ = mem.shape
    k, v = jnp.split(rmsnorm(mem, mem_norm_g) @ w_ckv, 2, axis=-1)
    k = rmsnorm(k.reshape(B, M, MEM_HEADS, MEM_HEAD_DIM), kg)
    return k, v.reshape(B, M, MEM_HEADS, MEM_HEAD_DIM)


def cross_attend(h, k, v, w_cq, qg, w_co):
    B, T, _ = h.shape
    q = rmsnorm((h @ w_cq).reshape(B, T, MEM_HEADS, MEM_HEAD_DIM), qg)
    s = jnp.einsum('bqhd,bmhd->bhqm', q, k).astype(jnp.float32) * (MEM_HEAD_DIM ** -0.5)
    p = jax.nn.softmax(s, axis=-1)
    o = jnp.einsum('bhqm,bmhd->bqhd', p.astype(v.dtype), v).reshape(B, T, D_MODEL)
    return o @ w_co


def conv_ffn(h, conv_state, w_up, conv_w, conv_b, w_down):
    T = h.shape[1]
    u = h @ w_up
    buf = jnp.concatenate([conv_state.astype(u.dtype), u], axis=1)
    c = conv_b
    for j in range(CONV_WIDTH):
        c = c + conv_w[j] * buf[:, j:j + T]
    gate, val = jnp.split(c, 2, axis=-1)
    return (jax.nn.silu(gate) * val) @ w_down, buf[:, T:]


def setup_inputs(seed: int = 0) -> dict:
    key = jax.random.key(seed)
    ks = iter(jax.random.split(key, 48))
    n_pages = PAST_LEN // PAGE_SIZE
    n_pool = DEC_BATCH * n_pages + (DEC_BATCH * n_pages) // 4
    f32 = jnp.float32

    def nrm(shape, scale=1.0):
        return scale * jax.random.normal(next(ks), shape, f32)

    def gain(shape):
        return 1.0 + 0.02 * jax.random.normal(next(ks), shape, f32)

    page_table = jax.random.permutation(next(ks), n_pool)[:DEC_BATCH * n_pages]
    page_table = page_table.reshape(DEC_BATCH, n_pages).astype(jnp.int32)
    return {
        "x_prompt": nrm((BATCH, SEQ, D_MODEL)),
        "x_sample": nrm((DEC_BATCH, DEC_SEQ, D_MODEL)),
        "cache_fox_k": nrm((DEPTH, n_pool, PAGE_SIZE, FOX_HEADS, HEAD_DIM)),
        "cache_fox_v": nrm((DEPTH, n_pool, PAGE_SIZE, FOX_HEADS, HEAD_DIM)),
        "cache_fox_logf": jax.nn.log_sigmoid(3.0 + nrm((DEPTH, n_pool, PAGE_SIZE, FOX_HEADS))),
        "cache_diff_k": nrm((DEPTH, n_pool, PAGE_SIZE, DIFF_HEADS, 2, HEAD_DIM)),
        "cache_diff_v": nrm((DEPTH, n_pool, PAGE_SIZE, DIFF_HEADS, DIFF_VDIM)),
        "cache_mem_k": nrm((DEPTH, DEC_BATCH, MEM_TOKENS, MEM_HEADS, MEM_HEAD_DIM)),
        "cache_mem_v": nrm((DEPTH, DEC_BATCH, MEM_TOKENS, MEM_HEADS, MEM_HEAD_DIM)),
        "state_ffn_conv": nrm((DEPTH, DEC_BATCH, CONV_WIDTH - 1, 2 * D_FF)),
        "page_table": page_table,
        "mem_prompt": nrm((BATCH, MEM_TOKENS, D_MODEL)),
        "attn_norm_g": gain((DEPTH, D_MODEL)),
        "w_in": nrm((DEPTH, D_MODEL, N_IN), D_MODEL ** -0.5),
        "b_forget": jax.random.uniform(next(ks), (DEPTH, FOX_HEADS), f32, 1.0, 6.0),
        "fox_q_norm_g": gain((DEPTH, HEAD_DIM)),
        "fox_k_norm_g": gain((DEPTH, HEAD_DIM)),
        "diff_q_norm_g": gain((DEPTH, HEAD_DIM)),
        "diff_k_norm_g": gain((DEPTH, HEAD_DIM)),
        "lambda_q1": nrm((DEPTH, HEAD_DIM), 0.1),
        "lambda_k1": nrm((DEPTH, HEAD_DIM), 0.1),
        "lambda_q2": nrm((DEPTH, HEAD_DIM), 0.1),
        "lambda_k2": nrm((DEPTH, HEAD_DIM), 0.1),
        "diff_subln_g": gain((DEPTH, DIFF_VDIM)),
        "w_out": nrm((DEPTH, MIX_WIDTH, D_MODEL), MIX_WIDTH ** -0.5),
        "cross_norm_g": gain((DEPTH, D_MODEL)),
        "mem_norm_g": gain((DEPTH, D_MODEL)),
        "w_cq": nrm((DEPTH, D_MODEL, D_MODEL), D_MODEL ** -0.5),
        "w_ckv": nrm((DEPTH, D_MODEL, 2 * D_MODEL), D_MODEL ** -0.5),
        "cross_q_norm_g": gain((DEPTH, MEM_HEAD_DIM)),
        "cross_k_norm_g": gain((DEPTH, MEM_HEAD_DIM)),
        "w_co": nrm((DEPTH, D_MODEL, D_MODEL), D_MODEL ** -0.5),
        "ffn_norm_g": gain((DEPTH, D_MODEL)),
        "w_up": nrm((DEPTH, D_MODEL, 2 * D_FF), D_MODEL ** -0.5),
        "conv_w": nrm((DEPTH, CONV_WIDTH, 2 * D_FF), CONV_WIDTH ** -0.5),
        "conv_b": nrm((DEPTH, 2 * D_FF), 0.01),
        "w_down": nrm((DEPTH, D_FF, D_MODEL), D_FF ** -0.5),
    }


def reference(x_prompt, x_sample, cache_fox_k, cache_fox_v, cache_fox_logf, cache_diff_k,
              cache_diff_v, cache_mem_k, cache_mem_v, state_ffn_conv, page_table, mem_prompt,
              attn_norm_g, w_in, b_forget, fox_q_norm_g, fox_k_norm_g, diff_q_norm_g,
              diff_k_norm_g, lambda_q1, lambda_k1, lambda_q2, lambda_k2, diff_subln_g, w_out,
              cross_norm_g, mem_norm_g, w_cq, w_ckv, cross_q_norm_g, cross_k_norm_g, w_co,
              ffn_norm_g, w_up, conv_w, conv_b, w_down):
    f32 = jnp.float32
    db = page_table.shape[0]
    pos_p = jnp.arange(SEQ)
    pos_s = PAST_LEN + jnp.arange(DEC_SEQ)
    pos_all = jnp.arange(PAST_LEN + DEC_SEQ)
    xp, xs = x_prompt, x_sample
    P = {n: [] for n in ('fk', 'fv', 'lf', 'dk', 'dv', 'mk', 'mv', 'cv')}
    S = {n: [] for n in ('fk', 'fv', 'lf', 'dk', 'dv', 'cv')}

    def past(cache, l):
        g = cache[l][page_table]
        return g.reshape((db, PAST_LEN) + cache.shape[3:])

    for l in range(DEPTH):
        lam_init = 0.8 - 0.6 * math.exp(-0.3 * l)
        lam = (jnp.exp(jnp.sum(lambda_q1[l].astype(f32) * lambda_k1[l].astype(f32)))
               - jnp.exp(jnp.sum(lambda_q2[l].astype(f32) * lambda_k2[l].astype(f32))) + lam_init)
        proj_w = (w_in[l], b_forget[l], fox_q_norm_g[l], fox_k_norm_g[l], diff_q_norm_g[l], diff_k_norm_g[l])

        h = rmsnorm(xp, attn_norm_g[l])
        fq, fk, fv, lf, fg, dq, dk, dv = mixer_proj(h, pos_p, *proj_w)
        c = jnp.cumsum(lf, axis=1)
        fo, do = prompt_sweep(fq, fk, fv, c, dq, dk, dv, lam)
        xp = xp + merge_groups(fo, fg, do, diff_subln_g[l], lam_init, w_out[l])
        mk, mv = mem_kv(mem_prompt, mem_norm_g[l], w_ckv[l], cross_k_norm_g[l])
        xp = xp + cross_attend(rmsnorm(xp, cross_norm_g[l]), mk, mv, w_cq[l], cross_q_norm_g[l], w_co[l])
        hf = rmsnorm(xp, ffn_norm_g[l])
        f, cv = conv_ffn(hf, jnp.zeros((hf.shape[0], CONV_WIDTH - 1, 2 * D_FF), hf.dtype),
                         w_up[l], conv_w[l], conv_b[l], w_down[l])
        xp = xp + f
        for n, a in (('fk', fk), ('fv', fv), ('lf', lf), ('dk', dk), ('dv', dv), ('mk', mk), ('mv', mv), ('cv', cv)):
            P[n].append(a)

        h = rmsnorm(xs, attn_norm_g[l])
        sq, sk, sv, slf, sg, sdq, sdk, sdv = mixer_proj(h, pos_s, *proj_w)
        fk_all = jnp.concatenate([past(cache_fox_k, l).astype(sk.dtype), sk], axis=1)
        fv_all = jnp.concatenate([past(cache_fox_v, l).astype(sv.dtype), sv], axis=1)
        c_all = jnp.cumsum(jnp.concatenate([past(cache_fox_logf, l).astype(f32), slf], axis=1), axis=1)
        dk_all = jnp.concatenate([past(cache_diff_k, l).astype(sdk.dtype), sdk], axis=1)
        dv_all = jnp.concatenate([past(cache_diff_v, l).astype(sdv.dtype), sdv], axis=1)
        fo = fox_attend(sq, c_all[:, PAST_LEN:], pos_s, fk_all, c_all, fv_all, pos_all)
        do = diff_attend(sdq, pos_s, dk_all, dv_all, pos_all, lam)
        xs = xs + merge_groups(fo, sg, do, diff_subln_g[l], lam_init, w_out[l])
        xs = xs + cross_attend(rmsnorm(xs, cross_norm_g[l]), cache_mem_k[l].astype(xs.dtype),
                               cache_mem_v[l].astype(xs.dtype), w_cq[l], cross_q_norm_g[l], w_co[l])
        f, scv = conv_ffn(rmsnorm(xs, ffn_norm_g[l]), state_ffn_conv[l], w_up[l], conv_w[l], conv_b[l], w_down[l])
        xs = xs + f
        for n, a in (('fk', sk), ('fv', sv), ('lf', slf), ('dk', sdk), ('dv', sdv), ('cv', scv)):
            S[n].append(a)

    st = lambda lst: jnp.stack(lst, axis=0)
    return (xp, xs,
            st(P['fk']), st(P['fv']), st(P['lf']), st(P['dk']), st(P['dv']), st(P['mk']), st(P['mv']), st(P['cv']),
            st(S['fk']), st(S['fv']), st(S['lf']), st(S['dk']), st(S['dv']), st(S['cv']))
```

```python
import functools
import math

import jax
import jax.numpy as jnp
import numpy as np
from jax import lax
from jax.experimental import pallas as pl
from jax.experimental.pallas import tpu as pltpu

F32 = jnp.float32
BF16 = jnp.bfloat16

HEAD_DIM = 64
FOX_HEADS = 8
DIFF_HEADS = 4
SEG = 512
FF_PAD = 128
MEM_HEADS = 4
MEM_HEAD_DIM = 256
CONV_WIDTH = 3
ROPE_THETA = 10000.0
EPS = 1e-6
NEG = -1e30
VMEM_LIMIT = 56 * 1024 * 1024

_NT = (((1,), (1,)), ((), ()))


def _dot(a, b):
    return jnp.dot(a, b, preferred_element_type=F32)


def _dot_nt(a, b):
    return lax.dot_general(a, b, _NT, preferred_element_type=F32)


def _split3(x):
    hi = x.astype(BF16)
    r = x - hi.astype(F32)
    mid = r.astype(BF16)
    lo = (r - mid.astype(F32)).astype(BF16)
    return hi, mid, lo


def _rms_rows(x, g):
    ms = jnp.mean(x * x, axis=-1, keepdims=True)
    return x * lax.rsqrt(ms + EPS) * g


def _cparams(sem):
    return pltpu.CompilerParams(dimension_semantics=sem, vmem_limit_bytes=VMEM_LIMIT)


def _proj_kernel(x_ref, g_ref, w_ref, bf_ref, gains_ref, cos_ref, sin_ref, gm_ref,
                 fq_ref, fk_ref, fkb_ref, fv_ref, fvb_ref, gate_ref,
                 dq_ref, dk_ref, dkb_ref, dv_ref, dvb_ref, lf_ref):
    tm = x_ref.shape[0]
    h = _rms_rows(x_ref[...], g_ref[...]).astype(BF16)
    gm = gm_ref[...]

    def seg(i):
        return _dot(h, w_ref[:, i * SEG:(i + 1) * SEG])

    def group_norm(z, gain):
        parts = []
        for c in range(SEG // 256):
            zc = z[:, c * 256:(c + 1) * 256]
            zz = zc * zc
            hi = zz.astype(BF16)
            lo = (zz - hi.astype(F32)).astype(BF16)
            ms = _dot(hi, gm) + _dot(lo, gm)
            parts.append(zc * lax.rsqrt(ms + EPS))
        return jnp.concatenate(parts, axis=1) * gain

    cos = jnp.concatenate([cos_ref[...]] * (SEG // 128), axis=1)
    sin = jnp.concatenate([sin_ref[...]] * (SEG // 128), axis=1)
    first_half = (lax.broadcasted_iota(jnp.int32, (tm, SEG), 1) & (HEAD_DIM // 2)) == 0

    def rope(y):
        partner = jnp.where(first_half,
                            pltpu.roll(y, SEG - HEAD_DIM // 2, 1),
                            pltpu.roll(y, HEAD_DIM // 2, 1))
        return y * cos + partner * sin

    fq_ref[...] = group_norm(seg(0), gains_ref[0:1, :]).astype(BF16)
    fk = group_norm(seg(1), gains_ref[1:2, :])
    fk_ref[...] = fk
    fkb_ref[...] = fk.astype(BF16)
    fv = seg(2)
    fv_ref[...] = fv
    fvb_ref[...] = fv.astype(BF16)
    gate_ref[...] = jax.nn.sigmoid(seg(3))
    dq_ref[...] = rope(group_norm(seg(4), gains_ref[2:3, :])).astype(BF16)
    dk = rope(group_norm(seg(5), gains_ref[3:4, :]))
    dk_ref[...] = dk
    dkb_ref[...] = dk.astype(BF16)
    dv = seg(6)
    dv_ref[...] = dv
    dvb_ref[...] = dv.astype(BF16)
    zf = _dot(h, w_ref[:, 7 * SEG:7 * SEG + FF_PAD]) + bf_ref[...]
    lf = jnp.minimum(zf, 0.0) - jnp.log1p(jnp.exp(-jnp.abs(zf)))
    lf_ref[...] = lf[:, :FOX_HEADS]


def _mixer_proj(x2d, g, w_prep, bf_pad, gains, cos_t, sin_t, gm, tm, n_tab_blocks):
    r, d = x2d.shape
    npad = w_prep.shape[1]
    const = lambda i: (0, 0)
    row = lambda i: (i, 0)
    tab = (lambda i: (i % n_tab_blocks, 0)) if n_tab_blocks > 1 else const
    wide_f32 = jax.ShapeDtypeStruct((r, SEG), F32)
    wide_bf = jax.ShapeDtypeStruct((r, SEG), BF16)
    out_shape = (wide_bf, wide_f32, wide_bf, wide_f32, wide_bf, wide_f32,
                 wide_bf, wide_f32, wide_bf, wide_f32, wide_bf,
                 jax.ShapeDtypeStruct((r, FOX_HEADS), F32))
    wide_spec = pl.BlockSpec((tm, SEG), row)
    out_specs = (wide_spec,) * 11 + (pl.BlockSpec((tm, FOX_HEADS), row),)
    return pl.pallas_call(
        _proj_kernel,
        grid=(r // tm,),
        in_specs=[
            pl.BlockSpec((tm, d), row),
            pl.BlockSpec((1, d), const),
            pl.BlockSpec((d, npad), const),
            pl.BlockSpec((1, FF_PAD), const),
            pl.BlockSpec((4, SEG), const),
            pl.BlockSpec((tm, 128), tab),
            pl.BlockSpec((tm, 128), tab),
            pl.BlockSpec((256, 256), const),
        ],
        out_specs=out_specs,
        out_shape=out_shape,
        compiler_params=_cparams(("arbitrary",)),
        name="mixer_proj",
    )(x2d, g, w_prep, bf_pad, gains, cos_t, sin_t, gm)


def _cumsum_kernel(x_ref, tri_ref, o_ref):
    rows, t = x_ref.shape
    tri = tri_ref[...]

    def body(c, carry):
        c0 = pl.multiple_of(c * 256, 256)
        hi, mid, lo = _split3(x_ref[:, pl.ds(c0, 256)])
        y = _dot(hi, tri) + _dot(mid, tri) + _dot(lo, tri) + carry
        o_ref[:, pl.ds(c0, 256)] = y
        return y[:, 255:256]

    lax.fori_loop(0, t // 256, body, jnp.zeros((rows, 1), F32))


def _cumsum_lanes(x, tri):
    rows, t = x.shape
    return pl.pallas_call(
        _cumsum_kernel,
        out_shape=jax.ShapeDtypeStruct((rows, t), F32),
        compiler_params=_cparams(None),
        name="logf_cumsum",
    )(x, tri)


def _lambda_value(lq1, lk1, lq2, lk2, lam_init):
    a = jnp.sum(lq1[...] * lk1[...], axis=-1, keepdims=True)
    b = jnp.sum(lq2[...] * lk2[...], axis=-1, keepdims=True)
    return jnp.exp(a) - jnp.exp(b) + lam_init


def _attn_body(q_ref, k_ref, v_ref, c_ref, tq, tk):
    qi = pl.program_id(2)
    q = q_ref[...]
    lane = lax.broadcasted_iota(jnp.int32, q.shape, 1)
    zero = jnp.zeros_like(q)
    qa = jnp.where(lane < HEAD_DIM, q, zero)
    qb = jnp.where(lane >= HEAD_DIM, q, zero)

    def update(s, v, m, l, acc):
        m_new = jnp.maximum(m, jnp.max(s, axis=1, keepdims=True))
        alpha = jnp.exp(m - m_new)
        p = jnp.exp(s - m_new)
        l = alpha * l + jnp.sum(p, axis=1, keepdims=True)
        acc = alpha * acc + _dot(p.astype(BF16), v)
        return m_new, l, acc

    def step(j, carry, masked):
        ma, la, acca, mb, lb, accb = carry
        k0 = pl.multiple_of(j * tk, tk)
        k = k_ref[pl.ds(k0, tk), :]
        v = v_ref[pl.ds(k0, tk), :]
        sa = _dot_nt(qa, k)
        sb = _dot_nt(qb, k)
        if c_ref is not None:
            sa = sa - c_ref[0:1, pl.ds(k0, tk)]
            sb = sb - c_ref[1:2, pl.ds(k0, tk)]
        if masked:
            keep = (lax.broadcasted_iota(jnp.int32, (tq, tk), 1)
                    <= lax.broadcasted_iota(jnp.int32, (tq, tk), 0))
            sa = jnp.where(keep, sa, NEG)
            sb = jnp.where(keep, sb, NEG)
        ma, la, acca = update(sa, v, ma, la, acca)
        mb, lb, accb = update(sb, v, mb, lb, accb)
        return ma, la, acca, mb, lb, accb

    m0 = jnp.full((tq, 1), NEG, F32)
    l0 = jnp.zeros((tq, 1), F32)
    a0 = jnp.zeros((tq, 128), F32)
    carry = lax.fori_loop(0, qi, functools.partial(step, masked=False), (m0, l0, a0, m0, l0, a0))
    ma, la, acca, mb, lb, accb = step(qi, carry, True)
    return acca / la, accb / lb, lane


def _fox_attn_kernel(q_ref, k_ref, v_ref, c_ref, gate_ref, o_ref, *, tq, tk):
    oa, ob, lane = _attn_body(q_ref, k_ref, v_ref, c_ref, tq, tk)
    o = jnp.where(lane < HEAD_DIM, oa, ob) * gate_ref[...]
    o_ref[...] = o.astype(o_ref.dtype)


def _diff_attn_kernel(q_ref, k_ref, v_ref, lq1, lk1, lq2, lk2, sub_ref, o_ref, *, tq, tk, lam_init):
    oa, ob, _ = _attn_body(q_ref, k_ref, v_ref, None, tq, tk)
    lam = _lambda_value(lq1, lk1, lq2, lk2, lam_init)
    d = oa - lam * ob
    o = _rms_rows(d, sub_ref[...]) * (1.0 - lam_init)
    o_ref[...] = o.astype(o_ref.dtype)


def _prompt_attention(kind, q, k, v, extra, t_blk, lam_init=None):
    b, t, _ = q.shape
    nblk = SEG // 128
    tq = tk = t_blk
    qspec = pl.BlockSpec((None, tq, 128), lambda bi, hi, qi: (bi, qi, hi))
    kvspec = pl.BlockSpec((None, t, 128), lambda bi, hi, qi: (bi, 0, hi))
    small = lambda shape: pl.BlockSpec(shape, lambda bi, hi, qi: (0,) * len(shape))
    if kind == "fox":
        c, gate = extra
        body = functools.partial(_fox_attn_kernel, tq=tq, tk=tk)
        in_specs = [qspec, kvspec, kvspec,
                    pl.BlockSpec((None, None, 2, t), lambda bi, hi, qi: (bi, hi, 0, 0)),
                    qspec]
        args = (q, k, v, c, gate)
    else:
        lq1, lk1, lq2, lk2, sub = extra
        body = functools.partial(_diff_attn_kernel, tq=tq, tk=tk, lam_init=lam_init)
        in_specs = [qspec, kvspec, kvspec] + [small((1, HEAD_DIM))] * 4 + [small((1, 128))]
        args = (q, k, v, lq1, lk1, lq2, lk2, sub)
    return pl.pallas_call(
        body,
        grid=(b, nblk, t // tq),
        in_specs=in_specs,
        out_specs=qspec,
        out_shape=jax.ShapeDtypeStruct((b, t, SEG), BF16),
        compiler_params=_cparams(("arbitrary", "arbitrary", "arbitrary")),
        name=kind + "_attention",
    )(*args)


def _merge_q_kernel(x_ref, fox_ref, diff_ref, wo_ref, g_ref, wq_ref, qg_ref, x1_ref, qn_ref):
    x1 = x_ref[...] + _dot(fox_ref[...], wo_ref[0:SEG, :]) + _dot(diff_ref[...], wo_ref[SEG:2 * SEG, :])
    x1_ref[...] = x1
    h = _rms_rows(x1, g_ref[...]).astype(BF16)
    for hd in range(MEM_HEADS):
        sl = slice(hd * MEM_HEAD_DIM, (hd + 1) * MEM_HEAD_DIM)
        qh = _dot(h, wq_ref[:, sl])
        qn_ref[:, sl] = _rms_rows(qh, qg_ref[...]).astype(BF16)


def _merge_q(x2d, fox, diff, w_out, g, w_cq, qg, tm):
    r, d = x2d.shape
    const = lambda i: (0, 0)
    row = lambda i: (i, 0)
    return pl.pallas_call(
        _merge_q_kernel,
        grid=(r // tm,),
        in_specs=[
            pl.BlockSpec((tm, d), row),
            pl.BlockSpec((tm, SEG), row),
            pl.BlockSpec((tm, SEG), row),
            pl.BlockSpec((2 * SEG, d), const),
            pl.BlockSpec((1, d), const),
            pl.BlockSpec((d, d), const),
            pl.BlockSpec((1, MEM_HEAD_DIM), const),
        ],
        out_specs=(pl.BlockSpec((tm, d), row), pl.BlockSpec((tm, d), row)),
        out_shape=(jax.ShapeDtypeStruct((r, d), F32), jax.ShapeDtypeStruct((r, d), BF16)),
        compiler_params=_cparams(("arbitrary",)),
        name="merge_q",
    )(x2d, fox, diff, w_out, g, w_cq, qg)


def _mem_kv_kernel(m_ref, g_ref, w_ref, kg_ref, k_ref, kb_ref, v_ref, vb_ref):
    d = m_ref.shape[1]
    h = _rms_rows(m_ref[...], g_ref[...]).astype(BF16)
    for hd in range(MEM_HEADS):
        sl = slice(hd * MEM_HEAD_DIM, (hd + 1) * MEM_HEAD_DIM)
        kh = _rms_rows(_dot(h, w_ref[:, sl]), kg_ref[...])
        k_ref[:, sl] = kh
        kb_ref[:, sl] = kh.astype(BF16)
    v = _dot(h, w_ref[:, d:2 * d])
    v_ref[...] = v
    vb_ref[...] = v.astype(BF16)


def _mem_kv(mem2d, g, w_ckv, kg, tm):
    r, d = mem2d.shape
    const = lambda i: (0, 0)
    row = lambda i: (i, 0)
    f = jax.ShapeDtypeStruct((r, d), F32)
    h = jax.ShapeDtypeStruct((r, d), BF16)
    spec = pl.BlockSpec((tm, d), row)
    return pl.pallas_call(
        _mem_kv_kernel,
        grid=(r // tm,),
        in_specs=[spec, pl.BlockSpec((1, d), const), pl.BlockSpec((d, 2 * d), const),
                  pl.BlockSpec((1, MEM_HEAD_DIM), const)],
        out_specs=(spec, spec, spec, spec),
        out_shape=(f, h, f, h),
        compiler_params=_cparams(("arbitrary",)),
        name="mem_kv",
    )(mem2d, g, w_ckv, kg)


def _cross_kernel(x1_ref, qn_ref, mk_ref, mv_ref, wo_ref, o_ref):
    acc = x1_ref[...]
    for hd in range(MEM_HEADS):
        sl = slice(hd * MEM_HEAD_DIM, (hd + 1) * MEM_HEAD_DIM)
        s = _dot_nt(qn_ref[:, sl], mk_ref[:, sl])
        m = jnp.max(s, axis=1, keepdims=True)
        p = jnp.exp(s - m)
        l = jnp.sum(p, axis=1, keepdims=True)
        oh = _dot(p.astype(BF16), mv_ref[:, sl]) / l
        acc = acc + _dot(oh.astype(BF16), wo_ref[sl, :])
    o_ref[...] = acc


def _cross_prompt(x1, qn, mkb, mvb, w_co, tm):
    b, t, d = x1.shape
    m = mkb.shape[1]
    rows = pl.BlockSpec((None, tm, d), lambda bi, i: (bi, i, 0))
    mem = pl.BlockSpec((None, m, d), lambda bi, i: (bi, 0, 0))
    return pl.pallas_call(
        _cross_kernel,
        grid=(b, t // tm),
        in_specs=[rows, rows, mem, mem, pl.BlockSpec((d, d), lambda bi, i: (0, 0))],
        out_specs=rows,
        out_shape=jax.ShapeDtypeStruct((b, t, d), F32),
        compiler_params=_cparams(("arbitrary", "arbitrary")),
        name="cross_prompt",
    )(x1, qn, mkb, mvb, w_co)


def _ffn_chunk(h, u_prev1, u_prev2, wu_ref, cw_ref, cb_ref, cols):
    u = _dot(h, wu_ref[:, cols])
    c = (cb_ref[:, cols] + cw_ref[0:1, cols] * u_prev2(u) + cw_ref[1:2, cols] * u_prev1(u)
         + cw_ref[2:3, cols] * u)
    return u, c


def _ffn_prompt_kernel(x_ref, g_ref, wu_ref, cw_ref, cb_ref, wd_ref, o_ref, tail_ref, prev_ref, *, cw):
    tm = x_ref.shape[0]
    dff = wd_ref.shape[0]
    ti = pl.program_id(1)

    @pl.when(ti == 0)
    def _():
        prev_ref[...] = jnp.zeros_like(prev_ref)

    x = x_ref[...]
    h = _rms_rows(x, g_ref[...]).astype(BF16)
    rowid = lax.broadcasted_iota(jnp.int32, (tm, cw), 0)
    acc = x
    for j in range(dff // cw):
        halves = []
        for base in (0, dff):
            cols = slice(base + j * cw, base + (j + 1) * cw)
            p1 = prev_ref[7:8, cols]
            p2 = prev_ref[6:7, cols]

            def prev1(u, p1=p1):
                return jnp.where(rowid == 0, p1, pltpu.roll(u, 1, 0))

            def prev2(u, p1=p1, p2=p2):
                return jnp.where(rowid == 0, p2, jnp.where(rowid == 1, p1, pltpu.roll(u, 2, 0)))

            u, c = _ffn_chunk(h, prev1, prev2, wu_ref, cw_ref, cb_ref, cols)
            prev_ref[:, cols] = u[tm - 8:tm, :]
            halves.append(c)
        a = (jax.nn.silu(halves[0]) * halves[1]).astype(BF16)
        acc = acc + _dot(a, wd_ref[j * cw:(j + 1) * cw, :])
    o_ref[...] = acc
    tail_ref[...] = prev_ref[6:8, :]


def _ffn_prompt(x, g, w_up, conv_w, conv_b, w_down, tm, cw):
    b, t, d = x.shape
    dff = w_down.shape[0]
    const = lambda bi, i: (0, 0)
    rows = pl.BlockSpec((None, tm, d), lambda bi, i: (bi, i, 0))
    return pl.pallas_call(
        functools.partial(_ffn_prompt_kernel, cw=cw),
        grid=(b, t // tm),
        in_specs=[rows, pl.BlockSpec((1, d), const),
                  pl.BlockSpec((d, 2 * dff), const, pipeline_mode=pl.Buffered(1)),
                  pl.BlockSpec((CONV_WIDTH, 2 * dff), const),
                  pl.BlockSpec((1, 2 * dff), const),
                  pl.BlockSpec((dff, d), const, pipeline_mode=pl.Buffered(1))],
        out_specs=(rows, pl.BlockSpec((None, CONV_WIDTH - 1, 2 * dff), lambda bi, i: (bi, 0, 0))),
        out_shape=(jax.ShapeDtypeStruct((b, t, d), F32),
                   jax.ShapeDtypeStruct((b, CONV_WIDTH - 1, 2 * dff), F32)),
        scratch_shapes=[pltpu.VMEM((8, 2 * dff), F32)],
        compiler_params=_cparams(("arbitrary", "arbitrary")),
        name="ffn_prompt",
    )(x, g, w_up, conv_w, conv_b, w_down)


def _ffn_sample_kernel(x_ref, g_ref, st_ref, wu_ref, cw_ref, cb_ref, wd_ref, o_ref, u_ref, *, cw):
    dff = wd_ref.shape[0]
    x = x_ref[...]
    h = _rms_rows(x, g_ref[...]).astype(BF16)
    acc = x
    for j in range(dff // cw):
        halves = []
        for base in (0, dff):
            cols = slice(base + j * cw, base + (j + 1) * cw)
            s0 = st_ref[0, :, cols]
            s1 = st_ref[1, :, cols]
            u, c = _ffn_chunk(h, lambda u, s1=s1: s1, lambda u, s0=s0: s0, wu_ref, cw_ref, cb_ref, cols)
            u_ref[:, cols] = u
            halves.append(c)
        a = (jax.nn.silu(halves[0]) * halves[1]).astype(BF16)
        acc = acc + _dot(a, wd_ref[j * cw:(j + 1) * cw, :])
    o_ref[...] = acc


def _ffn_sample(x, g, state_t, w_up, conv_w, conv_b, w_down, cw):
    r, d = x.shape
    dff = w_down.shape[0]
    return pl.pallas_call(
        functools.partial(_ffn_sample_kernel, cw=cw),
        out_shape=(jax.ShapeDtypeStruct((r, d), F32), jax.ShapeDtypeStruct((r, 2 * dff), F32)),
        compiler_params=_cparams(None),
        name="ffn_sample",
    )(x, g, state_t, w_up, conv_w, conv_b, w_down)


def _decode_kernel(pt_ref, qf_ref, qd_ref, knf_ref, vnf_ref, knd_ref, vnd_ref, gate_ref, slf_ref,
                   lq1, lk1, lq2, lk2, sub_ref, msuf_ref, ones_ref, *rest, pg, lam_init):
    fk = rest[0 * pg:1 * pg]
    fv = rest[1 * pg:2 * pg]
    dk = rest[2 * pg:3 * pg]
    dv = rest[3 * pg:4 * pg]
    lfp = rest[4 * pg:5 * pg]
    fo_ref, do_ref = rest[5 * pg:5 * pg + 2]
    mf_ref, lf_ref, af_ref, md_ref, ld_ref, ad_ref, carry_ref = rest[5 * pg + 2:]
    j = pl.program_id(1)
    nh = FOX_HEADS
    qf = qf_ref[...]
    qd = qd_ref[...]
    width = qf.shape[1]

    @pl.when(j == 0)
    def _():
        mf_ref[...] = jnp.sum(qf * knf_ref[...], axis=1, keepdims=True)
        lf_ref[...] = jnp.ones_like(lf_ref)
        af_ref[...] = jnp.broadcast_to(vnf_ref[...], af_ref.shape)
        md_ref[...] = jnp.sum(qd * knd_ref[...], axis=1, keepdims=True)
        ld_ref[...] = jnp.ones_like(ld_ref)
        ad_ref[...] = vnd_ref[...]
        carry_ref[...] = slf_ref[...]

    def online(scores, pv, m_ref, l_ref, a_ref):
        m_old = m_ref[...]
        smax = scores[0]
        for s in scores[1:]:
            smax = jnp.maximum(smax, s)
        m_new = jnp.maximum(m_old, jnp.max(smax, axis=1, keepdims=True))
        alpha = jnp.exp(m_old - m_new)
        acc = alpha * a_ref[...]
        psum = None
        for i, s in enumerate(scores):
            p = jnp.exp(s - m_new)
            psum = p if psum is None else psum + p
            acc = acc + pv(i, p.astype(BF16))
        m_ref[...] = m_new
        l_ref[...] = alpha * l_ref[...] + jnp.sum(psum, axis=1, keepdims=True)
        a_ref[...] = acc

    hi, mid, lo = _split3(jnp.concatenate([lfp[i][...] for i in range(pg)], axis=0))
    msuf = msuf_ref[...]
    ones = ones_ref[...]
    within = _dot(hi, msuf) + _dot(mid, msuf) + _dot(lo, msuf)
    total = _dot(hi, ones) + _dot(mid, ones) + _dot(lo, ones)
    carry = carry_ref[...]
    qfb = qf.astype(BF16)
    scores = []
    for i in range(pg):
        s = _dot(qfb, fk[i][...].reshape(width, -1).astype(BF16))
        scores.append(s + carry + within[i * nh:(i + 1) * nh, :])
        carry = carry + total[i * nh:(i + 1) * nh, :]
    carry_ref[...] = carry
    online(scores, lambda i, p: _dot_nt(p, fv[i][...].reshape(width, -1).astype(BF16)),
           mf_ref, lf_ref, af_ref)

    qdb = qd.astype(BF16)
    sub = lax.broadcasted_iota(jnp.int32, (2 * DIFF_HEADS, 1), 0)
    npg = dv[0].shape[0] // DIFF_HEADS

    def diff_pv(i, p):
        out = None
        for hd in range(DIFF_HEADS):
            ph = jnp.where(sub // 2 == hd, p, jnp.zeros_like(p))
            r = _dot(ph, dv[i][pl.ds(hd, npg, stride=DIFF_HEADS), :].astype(BF16))
            out = r if out is None else out + r
        return out

    scores = [_dot(qdb, dk[i][...].reshape(width, -1).astype(BF16)) for i in range(pg)]
    online(scores, diff_pv, md_ref, ld_ref, ad_ref)

    @pl.when(j == pl.num_programs(1) - 1)
    def _():
        o = af_ref[...] / lf_ref[...]
        own = (lax.broadcasted_iota(jnp.int32, o.shape, 1) // HEAD_DIM
               == lax.broadcasted_iota(jnp.int32, o.shape, 0))
        fo_ref[...] = jnp.sum(jnp.where(own, o, 0.0), axis=0, keepdims=True) * gate_ref[...]
        od = ad_ref[...] / ld_ref[...]
        lam = _lambda_value(lq1, lk1, lq2, lk2, lam_init)
        signed = jnp.where(sub % 2 == 0, od, -lam * od)
        d = signed + pltpu.roll(signed, 2 * DIFF_HEADS - 1, 0)
        do_ref[...] = _rms_rows(d, sub_ref[...]) * (1.0 - lam_init)


def _decode_attention(page_table, qf, qd, knf, vnf, knd, vnd, gate, slf_b, lam_params, sub, msuf, ones,
                      fox_k, fox_v, diff_k, diff_v, logf, pg, lam_init):
    ns, n_pages = page_table.shape
    ng = n_pages // pg
    page_size = logf.shape[2]
    width = qf.shape[2]
    seq = lambda shape: pl.BlockSpec((None,) + shape, lambda b, j, pt: (b,) + (0,) * len(shape))
    small = lambda shape: pl.BlockSpec(shape, lambda b, j, pt: (0,) * len(shape))

    def page(shape, i):
        return pl.BlockSpec((None,) + shape,
                            lambda b, j, pt: (pt[b, n_pages - 1 - (j * pg + i)],) + (0,) * len(shape))

    nh = FOX_HEADS
    in_specs = [seq((nh, width)), seq((nh, width)), seq((1, width)), seq((1, width)),
                seq((1, width)), seq((nh, 128)), seq((1, width)), seq((nh, page_size))]
    in_specs += [small((1, HEAD_DIM))] * 4 + [small((1, 128))]
    in_specs += [small((page_size, page_size))] * 2
    args = [qf, qd, knf, vnf, knd, vnd, gate, slf_b, *lam_params, sub, msuf, ones]
    for arr in (fox_k, fox_v, diff_k, diff_v, logf):
        for i in range(pg):
            in_specs.append(page(arr.shape[1:], i))
            args.append(arr)
    grid_spec = pltpu.PrefetchScalarGridSpec(
        num_scalar_prefetch=1,
        grid=(ns, ng),
        in_specs=in_specs,
        out_specs=(seq((1, width)), seq((nh, 128))),
        scratch_shapes=[pltpu.VMEM((nh, 1), F32), pltpu.VMEM((nh, 1), F32), pltpu.VMEM((nh, width), F32),
                        pltpu.VMEM((nh, 1), F32), pltpu.VMEM((nh, 1), F32), pltpu.VMEM((nh, 128), F32),
                        pltpu.VMEM((nh, page_size), F32)],
    )
    return pl.pallas_call(
        functools.partial(_decode_kernel, pg=pg, lam_init=lam_init),
        grid_spec=grid_spec,
        out_shape=(jax.ShapeDtypeStruct((ns, 1, width), F32),
                   jax.ShapeDtypeStruct((ns, nh, 128), F32)),
        compiler_params=_cparams(("arbitrary", "arbitrary")),
        name="decode_attention",
    )(page_table, *args)


def _cross_sample_kernel(q_ref, mk_ref, mv_ref, o_ref):
    q = q_ref[...]
    for hd in range(MEM_HEADS):
        k = mk_ref[:, hd, :].astype(BF16)
        v = mv_ref[:, hd, :].astype(BF16)
        s = _dot_nt(q, k)[hd:hd + 1, :]
        m = jnp.max(s, axis=1, keepdims=True)
        p = jnp.exp(s - m)
        l = jnp.sum(p, axis=1, keepdims=True)
        o_ref[hd:hd + 1, :] = _dot(p.astype(BF16), v) / l


def _cross_sample(qn, mem_k, mem_v):
    ns = qn.shape[0]
    m = mem_k.shape[1]
    qspec = pl.BlockSpec((None, MEM_HEADS, MEM_HEAD_DIM), lambda b: (b, 0, 0))
    mspec = pl.BlockSpec((None, m, MEM_HEADS, MEM_HEAD_DIM), lambda b: (b, 0, 0, 0))
    return pl.pallas_call(
        _cross_sample_kernel,
        grid=(ns,),
        in_specs=[qspec, mspec, mspec],
        out_specs=qspec,
        out_shape=jax.ShapeDtypeStruct((ns, MEM_HEADS, MEM_HEAD_DIM), F32),
        compiler_params=_cparams(("arbitrary",)),
        name="cross_sample",
    )(qn, mem_k, mem_v)


def _proj_res_kernel(x_ref, a_ref, w_ref, o_ref):
    o_ref[...] = x_ref[...] + _dot(a_ref[...].astype(BF16), w_ref[...])


def _proj_res(x, a, w):
    return pl.pallas_call(
        _proj_res_kernel,
        out_shape=jax.ShapeDtypeStruct(x.shape, F32),
        compiler_params=_cparams(None),
        name="proj_residual",
    )(x, a, w)


def _rope_tables(pos):
    half = HEAD_DIM // 2
    inv = ROPE_THETA ** (-2.0 * jnp.arange(half, dtype=F32) / HEAD_DIM)
    ang = pos.astype(F32)[:, None] * inv
    cos, sin = jnp.cos(ang), jnp.sin(ang)
    return jnp.tile(cos, (1, 4)), jnp.concatenate([-sin, sin, -sin, sin], axis=1)


def _group_mean_matrix():
    g = np.arange(256) // HEAD_DIM
    return jnp.asarray((g[:, None] == g[None, :]).astype(np.float32) / HEAD_DIM, dtype=BF16)


def _tri_matrix():
    i = np.arange(256)
    return jnp.asarray((i[:, None] <= i[None, :]).astype(np.float32), dtype=BF16)


def _page_matrices(page_size):
    i = np.arange(page_size)
    later = (i[:, None] > i[None, :]).astype(np.float32)
    return jnp.asarray(later, dtype=BF16), jnp.ones((page_size, page_size), BF16)


def _row_tile(r, pref):
    return pref if r % pref == 0 else r


def kernel(x_prompt, x_sample, cache_fox_k, cache_fox_v, cache_fox_logf, cache_diff_k, cache_diff_v, cache_mem_k, cache_mem_v, state_ffn_conv, page_table, mem_prompt, attn_norm_g, w_in, b_forget, fox_q_norm_g, fox_k_norm_g, diff_q_norm_g, diff_k_norm_g, lambda_q1, lambda_k1, lambda_q2, lambda_k2, diff_subln_g, w_out, cross_norm_g, mem_norm_g, w_cq, w_ckv, cross_q_norm_g, cross_k_norm_g, w_co, ffn_norm_g, w_up, conv_w, conv_b, w_down):
    depth = w_in.shape[0]
    assert depth == 1, "one layer per step"
    l = 0
    lam_init = 0.8 - 0.6 * math.exp(-0.3 * l)
    b, t, d = x_prompt.shape
    ns, ts, _ = x_sample.shape
    assert ts == 1
    n_pool, page_size = cache_fox_k.shape[1], cache_fox_k.shape[2]
    n_pages = page_table.shape[1]
    past_len = n_pages * page_size
    dff = w_down.shape[1]
    m_tok = mem_prompt.shape[1]

    sizes = (SEG, SEG, SEG, FOX_HEADS, SEG, SEG, SEG, SEG)
    offs = np.concatenate([[0], np.cumsum(sizes)])
    w = w_in[l]
    seg = lambda i: w[:, offs[i]:offs[i + 1]]
    w_prep = jnp.concatenate(
        [seg(0), seg(1), seg(2), seg(4), seg(5), seg(6), seg(7),
         jnp.pad(seg(3), ((0, 0), (0, FF_PAD - FOX_HEADS)))], axis=1).astype(BF16)
    bf_pad = jnp.pad(b_forget[l], (0, FF_PAD - FOX_HEADS))[None, :]
    scale = HEAD_DIM ** -0.5
    tile8 = lambda g: jnp.tile(g, SEG // HEAD_DIM)
    gains = jnp.stack([tile8(fox_q_norm_g[l]) * scale, tile8(fox_k_norm_g[l]),
                       tile8(diff_q_norm_g[l]) * scale, tile8(diff_k_norm_g[l])])
    gm = _group_mean_matrix()
    g_attn = attn_norm_g[l][None, :]
    lam_params = (lambda_q1[l][None, :], lambda_k1[l][None, :], lambda_q2[l][None, :], lambda_k2[l][None, :])
    sub_g = diff_subln_g[l][None, :]
    w_out_b = w_out[l].astype(BF16)
    w_cq_b = w_cq[l].astype(BF16)
    w_co_b = w_co[l].astype(BF16)
    w_ckv_b = w_ckv[l].astype(BF16)
    w_up_b = w_up[l].astype(BF16)
    w_down_b = w_down[l].astype(BF16)
    g_cross = cross_norm_g[l][None, :]
    qg = (cross_q_norm_g[l] * MEM_HEAD_DIM ** -0.5)[None, :]
    kg = cross_k_norm_g[l][None, :]
    g_ffn = ffn_norm_g[l][None, :]
    cw_l = conv_w[l]
    cb_l = conv_b[l][None, :]
    ffn_cw = 256 if dff % 256 == 0 else 128

    tm = _row_tile(t, 256)
    cos_p, sin_p = _rope_tables(jnp.arange(t))
    xp2d = x_prompt.reshape(b * t, d)
    (fq, fk, fkb, fv, fvb, gate, dq, dk, dkb, dv, dvb, lf) = _mixer_proj(
        xp2d, g_attn, w_prep, bf_pad, gains, cos_p, sin_p, gm, tm, t // tm)

    lf_rows = jnp.transpose(lf.reshape(b, t, FOX_HEADS), (0, 2, 1)).reshape(b * FOX_HEADS, t)
    c_rows = _cumsum_lanes(lf_rows, _tri_matrix()).reshape(b, FOX_HEADS // 2, 2, t)

    t_blk = _row_tile(t, 512)
    r3 = lambda a: a.reshape(b, t, SEG)
    fox_o = _prompt_attention("fox", r3(fq), r3(fkb), r3(fvb), (c_rows, r3(gate)), t_blk)
    diff_o = _prompt_attention("diff", r3(dq), r3(dkb), r3(dvb), lam_params + (sub_g,), t_blk, lam_init)

    x1, qn = _merge_q(xp2d, fox_o.reshape(b * t, SEG), diff_o.reshape(b * t, SEG),
                      w_out_b, g_cross, w_cq_b, qg, tm)
    mk, mkb, mv, mvb = _mem_kv(mem_prompt.reshape(b * m_tok, d), mem_norm_g[l][None, :], w_ckv_b, kg,
                               _row_tile(b * m_tok, 256))
    x2 = _cross_prompt(x1.reshape(b, t, d), qn.reshape(b, t, d),
                       mkb.reshape(b, m_tok, d), mvb.reshape(b, m_tok, d), w_co_b, tm)
    y_prompt, p_conv = _ffn_prompt(x2, g_ffn, w_up_b, cw_l, cb_l, w_down_b, tm, ffn_cw)

    cos_s, sin_s = _rope_tables(jnp.full((ns,), past_len))
    xs2d = x_sample.reshape(ns, d)
    (sq, sk, _, sv, _, sgate, sdq, sdk, _, sdv, _, slf) = _mixer_proj(
        xs2d, g_attn, w_prep, bf_pad, gains, cos_s, sin_s, gm, ns, 1)

    eye = jnp.eye(FOX_HEADS, dtype=F32)
    block_diag = lambda q: jnp.einsum("nhd,hg->nhgd", q.astype(F32).reshape(ns, FOX_HEADS, HEAD_DIM),
                                      eye).reshape(ns, FOX_HEADS, SEG)
    row3 = lambda a: a.reshape(ns, 1, SEG)
    msuf, ones = _page_matrices(page_size)
    fmajor = lambda c: jnp.moveaxis(c.reshape(n_pool, page_size, FOX_HEADS, HEAD_DIM), 1, 3)
    fo_s, do_s = _decode_attention(
        page_table, block_diag(sq), block_diag(sdq), row3(sk), row3(sv), row3(sdk),
        jnp.repeat(sdv.reshape(ns, DIFF_HEADS, 128), 2, axis=1), row3(sgate),
        jnp.broadcast_to(slf[:, :, None], (ns, FOX_HEADS, page_size)),
        lam_params, sub_g, msuf, ones,
        fmajor(cache_fox_k[l]), fmajor(cache_fox_v[l]), fmajor(cache_diff_k[l]),
        cache_diff_v[l].reshape(n_pool, page_size * DIFF_HEADS, 128),
        jnp.moveaxis(cache_fox_logf[l], 1, 2),
        8 if n_pages % 8 == 0 else 1, lam_init)
    do_s = do_s[:, ::2, :]

    xs1, sqn = _merge_q(xs2d, fo_s.reshape(ns, SEG).astype(BF16), do_s.reshape(ns, SEG).astype(BF16),
                        w_out_b, g_cross, w_cq_b, qg, ns)
    co = _cross_sample(sqn.reshape(ns, MEM_HEADS, MEM_HEAD_DIM), cache_mem_k[l], cache_mem_v[l])
    xs2 = _proj_res(xs1, co.reshape(ns, d), w_co_b)
    y_sample, su = _ffn_sample(xs2, g_ffn, jnp.transpose(state_ffn_conv[l], (1, 0, 2)),
                               w_up_b, cw_l, cb_l, w_down_b, ffn_cw)
    s_conv = jnp.stack([state_ffn_conv[l][:, 1, :], su], axis=1)

    st = lambda a, shape: a.reshape((1,) + shape)
    return (
        y_prompt, y_sample.reshape(ns, 1, d),
        st(fk, (b, t, FOX_HEADS, HEAD_DIM)), st(fv, (b, t, FOX_HEADS, HEAD_DIM)),
        st(lf, (b, t, FOX_HEADS)),
        st(dk, (b, t, DIFF_HEADS, 2, HEAD_DIM)), st(dv, (b, t, DIFF_HEADS, 2 * HEAD_DIM)),
        st(mk, (b, m_tok, MEM_HEADS, MEM_HEAD_DIM)), st(mv, (b, m_tok, MEM_HEADS, MEM_HEAD_DIM)),
        st(p_conv, (b, CONV_WIDTH - 1, 2 * dff)),
        st(sk, (ns, 1, FOX_HEADS, HEAD_DIM)), st(sv, (ns, 1, FOX_HEADS, HEAD_DIM)),
        st(slf, (ns, 1, FOX_HEADS)),
        st(sdk, (ns, 1, DIFF_HEADS, 2, HEAD_DIM)), st(sdv, (ns, 1, DIFF_HEADS, 2 * HEAD_DIM)),
        st(s_conv, (ns, CONV_WIDTH - 1, 2 * dff)),
    )
```

```python
import functools
import math

import jax
import jax.numpy as jnp
import numpy as np
from jax import lax
from jax.experimental import pallas as pl
from jax.experimental.pallas import tpu as pltpu

F32 = jnp.float32
BF16 = jnp.bfloat16

HEAD_DIM = 64
FOX_HEADS = 8
DIFF_HEADS = 4
SEG = 512
FF_PAD = 128
MEM_HEADS = 4
MEM_HEAD_DIM = 256
CONV_WIDTH = 3
ROPE_THETA = 10000.0
EPS = 1e-6
NEG = -1e30
LOG2E = math.log2(math.e)
BOUND_LIMIT = 56.0
VMEM_LIMIT = 56 * 1024 * 1024

_NT = (((1,), (1,)), ((), ()))


def _dot(a, b):
    return jnp.dot(a, b, preferred_element_type=F32)


def _dot_nt(a, b):
    return lax.dot_general(a, b, _NT, preferred_element_type=F32)


def _split3(x):
    hi = x.astype(BF16)
    r = x - hi.astype(F32)
    mid = r.astype(BF16)
    lo = (r - mid.astype(F32)).astype(BF16)
    return hi, mid, lo


def _rms_rows(x, g):
    ms = jnp.mean(x * x, axis=-1, keepdims=True)
    return x * lax.rsqrt(ms + EPS) * g


def _cparams(sem):
    return pltpu.CompilerParams(dimension_semantics=sem, vmem_limit_bytes=VMEM_LIMIT)


def _proj_kernel(x_ref, g_ref, w_ref, bf_ref, gains_ref, cos_ref, sin_ref, gm_ref,
                 fq_ref, fk_ref, fkb_ref, fv_ref, fvb_ref, gate_ref,
                 dq_ref, dk_ref, dkb_ref, dv_ref, dvb_ref, lf_ref, *, feature_major):
    def put(ref, val):
        ref[...] = val.T if feature_major else val

    tm = x_ref.shape[0]
    h = _rms_rows(x_ref[...], g_ref[...]).astype(BF16)
    gm = gm_ref[...]

    def seg(i):
        return _dot(h, w_ref[:, i * SEG:(i + 1) * SEG])

    def group_norm(z, gain):
        parts = []
        for c in range(SEG // 256):
            zc = z[:, c * 256:(c + 1) * 256]
            zz = zc * zc
            hi = zz.astype(BF16)
            lo = (zz - hi.astype(F32)).astype(BF16)
            ms = _dot(hi, gm) + _dot(lo, gm)
            parts.append(zc * lax.rsqrt(ms + EPS))
        return jnp.concatenate(parts, axis=1) * gain

    cos = jnp.concatenate([cos_ref[...]] * (SEG // 128), axis=1)
    sin = jnp.concatenate([sin_ref[...]] * (SEG // 128), axis=1)
    first_half = (lax.broadcasted_iota(jnp.int32, (tm, SEG), 1) & (HEAD_DIM // 2)) == 0

    def rope(y):
        partner = jnp.where(first_half,
                            pltpu.roll(y, SEG - HEAD_DIM // 2, 1),
                            pltpu.roll(y, HEAD_DIM // 2, 1))
        return y * cos + partner * sin

    fq_ref[...] = group_norm(seg(0), gains_ref[0:1, :]).astype(BF16)
    fk = group_norm(seg(1), gains_ref[1:2, :])
    put(fk_ref, fk)
    fkb_ref[...] = fk.astype(BF16)
    fv = seg(2)
    put(fv_ref, fv)
    fvb_ref[...] = fv.astype(BF16)
    gate_ref[...] = jax.nn.sigmoid(seg(3))
    dq_ref[...] = rope(group_norm(seg(4), gains_ref[2:3, :])).astype(BF16)
    dk = rope(group_norm(seg(5), gains_ref[3:4, :]))
    put(dk_ref, dk)
    dkb_ref[...] = dk.astype(BF16)
    dv = seg(6)
    dv_ref[...] = dv
    dvb_ref[...] = dv.astype(BF16)
    zf = _dot(h, w_ref[:, 7 * SEG:7 * SEG + FF_PAD]) + bf_ref[...]
    lf = jnp.minimum(zf, 0.0) - jnp.log1p(jnp.exp(-jnp.abs(zf)))
    lf_ref[...] = lf[:, :FOX_HEADS]


def _mixer_proj(x2d, g, w_prep, bf_pad, gains, cos_t, sin_t, gm, tm, n_tab_blocks, feature_major):
    r, d = x2d.shape
    npad = w_prep.shape[1]
    const = lambda i: (0, 0)
    row = lambda i: (i, 0)
    tab = (lambda i: (i % n_tab_blocks, 0)) if n_tab_blocks > 1 else const
    wide_f32 = jax.ShapeDtypeStruct((r, SEG), F32)
    wide_bf = jax.ShapeDtypeStruct((r, SEG), BF16)
    wide_spec = pl.BlockSpec((tm, SEG), row)
    if feature_major:
        kv_f32 = jax.ShapeDtypeStruct((r // (tm * n_tab_blocks), SEG, tm * n_tab_blocks), F32)
        kv_spec = pl.BlockSpec((None, SEG, tm), lambda i: (i // n_tab_blocks, 0, i % n_tab_blocks))
    else:
        kv_f32, kv_spec = wide_f32, wide_spec
    out_shape = (wide_bf, kv_f32, wide_bf, kv_f32, wide_bf, wide_f32,
                 wide_bf, kv_f32, wide_bf, wide_f32, wide_bf,
                 jax.ShapeDtypeStruct((r, FOX_HEADS), F32))
    out_specs = (wide_spec, kv_spec, wide_spec, kv_spec, wide_spec, wide_spec,
                 wide_spec, kv_spec, wide_spec, wide_spec, wide_spec,
                 pl.BlockSpec((tm, FOX_HEADS), row))
    return pl.pallas_call(
        functools.partial(_proj_kernel, feature_major=feature_major),
        grid=(r // tm,),
        in_specs=[
            pl.BlockSpec((tm, d), row),
            pl.BlockSpec((1, d), const),
            pl.BlockSpec((d, npad), const),
            pl.BlockSpec((1, FF_PAD), const),
            pl.BlockSpec((4, SEG), const),
            pl.BlockSpec((tm, 128), tab),
            pl.BlockSpec((tm, 128), tab),
            pl.BlockSpec((256, 256), const),
        ],
        out_specs=out_specs,
        out_shape=out_shape,
        compiler_params=_cparams(("arbitrary",)),
        name="mixer_proj",
    )(x2d, g, w_prep, bf_pad, gains, cos_t, sin_t, gm)


def _cumsum_kernel(x_ref, tri_ref, o_ref):
    rows, t = x_ref.shape
    tri = tri_ref[...]

    def body(c, carry):
        c0 = pl.multiple_of(c * 256, 256)
        hi, mid, lo = _split3(x_ref[:, pl.ds(c0, 256)])
        y = _dot(hi, tri) + _dot(mid, tri) + _dot(lo, tri) + carry
        o_ref[:, pl.ds(c0, 256)] = y * LOG2E
        return y[:, 255:256]

    lax.fori_loop(0, t // 256, body, jnp.zeros((rows, 1), F32))


def _cumsum_lanes(x, tri):
    rows, t = x.shape
    return pl.pallas_call(
        _cumsum_kernel,
        out_shape=jax.ShapeDtypeStruct((rows, t), F32),
        compiler_params=_cparams(None),
        name="logf_cumsum",
    )(x, tri)


def _lambda_value(lq1, lk1, lq2, lk2, lam_init):
    a = jnp.sum(lq1[...] * lk1[...], axis=-1, keepdims=True)
    b = jnp.sum(lq2[...] * lk2[...], axis=-1, keepdims=True)
    return jnp.exp(a) - jnp.exp(b) + lam_init


def _split_maps(q_ref):
    q = q_ref[...]
    lane = lax.broadcasted_iota(jnp.int32, q.shape, 1)
    zero = jnp.zeros_like(q)
    return jnp.where(lane < HEAD_DIM, q, zero), jnp.where(lane >= HEAD_DIM, q, zero), lane


def _attn_online(qi, q_ref, k_ref, v_ref, crow_ref, tq, tk):
    qa, qb, _ = _split_maps(q_ref)

    def update(s, v, m, l, acc):
        m_new = jnp.maximum(m, jnp.max(s, axis=1, keepdims=True))
        alpha = jnp.exp2(m - m_new)
        p = jnp.exp2(s - m_new)
        l = alpha * l + jnp.sum(p, axis=1, keepdims=True)
        acc = alpha * acc + _dot(p.astype(BF16), v)
        return m_new, l, acc

    def step(j, carry, masked):
        ma, la, acca, mb, lb, accb = carry
        k0 = pl.multiple_of(j * tk, tk)
        k = k_ref[pl.ds(k0, tk), :]
        v = v_ref[pl.ds(k0, tk), :]
        sa = _dot_nt(qa, k)
        sb = _dot_nt(qb, k)
        if crow_ref is not None:
            sa = sa - crow_ref[0:1, pl.ds(k0, tk)]
            sb = sb - crow_ref[1:2, pl.ds(k0, tk)]
        if masked:
            keep = (lax.broadcasted_iota(jnp.int32, (tq, tk), 1)
                    <= lax.broadcasted_iota(jnp.int32, (tq, tk), 0))
            sa = jnp.where(keep, sa, NEG)
            sb = jnp.where(keep, sb, NEG)
        ma, la, acca = update(sa, v, ma, la, acca)
        mb, lb, accb = update(sb, v, mb, lb, accb)
        return ma, la, acca, mb, lb, accb

    m0 = jnp.full((tq, 1), NEG, F32)
    l0 = jnp.zeros((tq, 1), F32)
    a0 = jnp.zeros((tq, 128), F32)
    carry = lax.fori_loop(0, qi, functools.partial(step, masked=False), (m0, l0, a0, m0, l0, a0))
    ma, la, acca, mb, lb, accb = step(qi, carry, True)
    return acca / la, accb / lb


def _attn_bounded(qi, q_ref, k_ref, v_ref, crow_ref, rowa, rowb, tq, tk):
    qa, qb, _ = _split_maps(q_ref)
    nchunk = tk // 128
    keep128 = (lax.broadcasted_iota(jnp.int32, (tq, 128), 1)
               <= lax.broadcasted_iota(jnp.int32, (tq, 128), 0))
    row_id = lax.broadcasted_iota(jnp.int32, (tq, 128), 0)

    def one_map(s, row, col, v, l, acc, masked):
        ps = []
        for c in range(nchunk):
            t = s[:, c * 128:(c + 1) * 128] + row
            if col is not None:
                t = t + col[:, c * 128:(c + 1) * 128]
            if masked:
                t = jnp.where(keep128 if c == 0 else
                              (lax.broadcasted_iota(jnp.int32, (tq, 128), 1) + c * 128 <= row_id), t, NEG)
            p = jnp.exp2(t)
            l = l + p
            ps.append(p.astype(BF16))
        acc = acc + _dot(jnp.concatenate(ps, axis=1), v)
        return l, acc

    def step(j, carry, masked):
        la, acca, lb, accb = carry
        k0 = pl.multiple_of(j * tk, tk)
        k = k_ref[pl.ds(k0, tk), :]
        v = v_ref[pl.ds(k0, tk), :]
        cola = colb = None
        if crow_ref is not None:
            cola = -crow_ref[0:1, pl.ds(k0, tk)]
            colb = -crow_ref[1:2, pl.ds(k0, tk)]
        la, acca = one_map(_dot_nt(qa, k), rowa, cola, v, la, acca, masked)
        lb, accb = one_map(_dot_nt(qb, k), rowb, colb, v, lb, accb, masked)
        return la, acca, lb, accb

    z = jnp.zeros((tq, 128), F32)
    carry = lax.fori_loop(0, qi, functools.partial(step, masked=False), (z, z, z, z))
    la, acca, lb, accb = step(qi, carry, True)
    return (acca / jnp.sum(la, axis=1, keepdims=True), accb / jnp.sum(lb, axis=1, keepdims=True))


def _fox_attn_kernel(fast_ref, bnd_ref, q_ref, k_ref, v_ref, crow_ref, ccol_ref, gate_ref, o_ref, *, tq, tk):
    hp = pl.program_id(1)
    qi = pl.program_id(2)
    fast = fast_ref[0, 0]
    lane = lax.broadcasted_iota(jnp.int32, (tq, 128), 1)

    def finish(oa, ob):
        o = jnp.where(lane < HEAD_DIM, oa, ob) * gate_ref[...]
        o_ref[...] = o.astype(o_ref.dtype)

    @pl.when(fast == 1)
    def _():
        hi, mid, lo = _split3(ccol_ref[...])
        src = lax.broadcasted_iota(jnp.int32, (128, 128), 0)
        bound = bnd_ref[...]

        def pick(h):
            sel = jnp.where(src == h, 1.0, 0.0).astype(BF16)
            return _dot(hi, sel) + _dot(mid, sel) + _dot(lo, sel) - bound

        finish(*_attn_bounded(qi, q_ref, k_ref, v_ref, crow_ref, pick(2 * hp), pick(2 * hp + 1), tq, tk))

    @pl.when(fast != 1)
    def _():
        finish(*_attn_online(qi, q_ref, k_ref, v_ref, crow_ref, tq, tk))


def _diff_attn_kernel(fast_ref, bnd_ref, q_ref, k_ref, v_ref, lq1, lk1, lq2, lk2, sub_ref, o_ref, *,
                      tq, tk, lam_init):
    qi = pl.program_id(2)
    fast = fast_ref[0, 0]

    def finish(oa, ob):
        lam = _lambda_value(lq1, lk1, lq2, lk2, lam_init)
        d = oa - lam * ob
        o = _rms_rows(d, sub_ref[...]) * (1.0 - lam_init)
        o_ref[...] = o.astype(o_ref.dtype)

    @pl.when(fast == 1)
    def _():
        row = -bnd_ref[...]
        finish(*_attn_bounded(qi, q_ref, k_ref, v_ref, None, row, row, tq, tk))

    @pl.when(fast != 1)
    def _():
        finish(*_attn_online(qi, q_ref, k_ref, v_ref, None, tq, tk))


def _prompt_attention(kind, bound, q, k, v, extra, t_blk, lam_init=None):
    b, t, _ = q.shape
    nblk = SEG // 128
    tq = tk = t_blk
    qspec = pl.BlockSpec((None, tq, 128), lambda bi, hi, qi: (bi, qi, hi))
    kvspec = pl.BlockSpec((None, t, 128), lambda bi, hi, qi: (bi, 0, hi))
    small = lambda shape: pl.BlockSpec(shape, lambda bi, hi, qi: (0,) * len(shape))
    smem = pl.BlockSpec(memory_space=pltpu.SMEM)
    fast = (bound <= BOUND_LIMIT).astype(jnp.int32).reshape(1, 1)
    bound_row = jnp.broadcast_to(bound.astype(F32).reshape(1, 1), (1, 128))
    if kind == "fox":
        crow, ccol, gate = extra
        body = functools.partial(_fox_attn_kernel, tq=tq, tk=tk)
        in_specs = [smem, small((1, 128)), qspec, kvspec, kvspec,
                    pl.BlockSpec((None, None, 2, t), lambda bi, hi, qi: (bi, hi, 0, 0)),
                    pl.BlockSpec((None, tq, 128), lambda bi, hi, qi: (bi, qi, 0)),
                    qspec]
        args = (fast, bound_row, q, k, v, crow, ccol, gate)
    else:
        lq1, lk1, lq2, lk2, sub = extra
        body = functools.partial(_diff_attn_kernel, tq=tq, tk=tk, lam_init=lam_init)
        in_specs = [smem, small((1, 128)), qspec, kvspec, kvspec] + [small((1, HEAD_DIM))] * 4 + [small((1, 128))]
        args = (fast, bound_row, q, k, v, lq1, lk1, lq2, lk2, sub)
    return pl.pallas_call(
        body,
        grid=(b, nblk, t // tq),
        in_specs=in_specs,
        out_specs=qspec,
        out_shape=jax.ShapeDtypeStruct((b, t, SEG), BF16),
        compiler_params=_cparams(("arbitrary", "arbitrary", "arbitrary")),
        name=kind + "_attention",
    )(*args)


def _merge_q_kernel(x_ref, fox_ref, diff_ref, wo_ref, g_ref, wq_ref, qg_ref, x1_ref, qn_ref):
    x1 = x_ref[...] + _dot(fox_ref[...], wo_ref[0:SEG, :]) + _dot(diff_ref[...], wo_ref[SEG:2 * SEG, :])
    x1_ref[...] = x1
    h = _rms_rows(x1, g_ref[...]).astype(BF16)
    for hd in range(MEM_HEADS):
        sl = slice(hd * MEM_HEAD_DIM, (hd + 1) * MEM_HEAD_DIM)
        qh = _dot(h, wq_ref[:, sl])
        qn_ref[:, sl] = _rms_rows(qh, qg_ref[...]).astype(BF16)


def _merge_q(x2d, fox, diff, w_out, g, w_cq, qg, tm):
    r, d = x2d.shape
    const = lambda i: (0, 0)
    row = lambda i: (i, 0)
    return pl.pallas_call(
        _merge_q_kernel,
        grid=(r // tm,),
        in_specs=[
            pl.BlockSpec((tm, d), row),
            pl.BlockSpec((tm, SEG), row),
            pl.BlockSpec((tm, SEG), row),
            pl.BlockSpec((2 * SEG, d), const),
            pl.BlockSpec((1, d), const),
            pl.BlockSpec((d, d), const),
            pl.BlockSpec((1, MEM_HEAD_DIM), const),
        ],
        out_specs=(pl.BlockSpec((tm, d), row), pl.BlockSpec((tm, d), row)),
        out_shape=(jax.ShapeDtypeStruct((r, d), F32), jax.ShapeDtypeStruct((r, d), BF16)),
        compiler_params=_cparams(("arbitrary",)),
        name="merge_q",
    )(x2d, fox, diff, w_out, g, w_cq, qg)


def _mem_kv_kernel(m_ref, g_ref, w_ref, kg_ref, k_ref, kb_ref, v_ref, vb_ref):
    d = m_ref.shape[1]
    h = _rms_rows(m_ref[...], g_ref[...]).astype(BF16)
    for hd in range(MEM_HEADS):
        sl = slice(hd * MEM_HEAD_DIM, (hd + 1) * MEM_HEAD_DIM)
        kh = _rms_rows(_dot(h, w_ref[:, sl]), kg_ref[...])
        k_ref[:, sl] = kh
        kb_ref[:, sl] = kh.astype(BF16)
    v = _dot(h, w_ref[:, d:2 * d])
    v_ref[...] = v
    vb_ref[...] = v.astype(BF16)


def _mem_kv(mem2d, g, w_ckv, kg, tm):
    r, d = mem2d.shape
    const = lambda i: (0, 0)
    row = lambda i: (i, 0)
    f = jax.ShapeDtypeStruct((r, d), F32)
    h = jax.ShapeDtypeStruct((r, d), BF16)
    spec = pl.BlockSpec((tm, d), row)
    return pl.pallas_call(
        _mem_kv_kernel,
        grid=(r // tm,),
        in_specs=[spec, pl.BlockSpec((1, d), const), pl.BlockSpec((d, 2 * d), const),
                  pl.BlockSpec((1, MEM_HEAD_DIM), const)],
        out_specs=(spec, spec, spec, spec),
        out_shape=(f, h, f, h),
        compiler_params=_cparams(("arbitrary",)),
        name="mem_kv",
    )(mem2d, g, w_ckv, kg)


def _cross_kernel(x1_ref, qn_ref, mk_ref, mv_ref, wo_ref, o_ref):
    acc = x1_ref[...]
    for hd in range(MEM_HEADS):
        sl = slice(hd * MEM_HEAD_DIM, (hd + 1) * MEM_HEAD_DIM)
        s = _dot_nt(qn_ref[:, sl], mk_ref[:, sl])
        m = jnp.max(s, axis=1, keepdims=True)
        p = jnp.exp(s - m)
        l = jnp.sum(p, axis=1, keepdims=True)
        oh = _dot(p.astype(BF16), mv_ref[:, sl]) / l
        acc = acc + _dot(oh.astype(BF16), wo_ref[sl, :])
    o_ref[...] = acc


def _cross_prompt(x1, qn, mkb, mvb, w_co, tm):
    b, t, d = x1.shape
    m = mkb.shape[1]
    rows = pl.BlockSpec((None, tm, d), lambda bi, i: (bi, i, 0))
    mem = pl.BlockSpec((None, m, d), lambda bi, i: (bi, 0, 0))
    return pl.pallas_call(
        _cross_kernel,
        grid=(b, t // tm),
        in_specs=[rows, rows, mem, mem, pl.BlockSpec((d, d), lambda bi, i: (0, 0))],
        out_specs=rows,
        out_shape=jax.ShapeDtypeStruct((b, t, d), F32),
        compiler_params=_cparams(("arbitrary", "arbitrary")),
        name="cross_prompt",
    )(x1, qn, mkb, mvb, w_co)


def _ffn_chunk(h, u_prev1, u_prev2, wu_ref, cw_ref, cb_ref, cols):
    u = _dot(h, wu_ref[:, cols])
    c = (cb_ref[:, cols] + cw_ref[0:1, cols] * u_prev2(u) + cw_ref[1:2, cols] * u_prev1(u)
         + cw_ref[2:3, cols] * u)
    return u, c


def _ffn_prompt_kernel(x_ref, g_ref, wu_ref, cw_ref, cb_ref, wd_ref, o_ref, tail_ref, prev_ref, *, cw):
    tm = x_ref.shape[0]
    dff = wd_ref.shape[0]
    ti = pl.program_id(1)

    @pl.when(ti == 0)
    def _():
        prev_ref[...] = jnp.zeros_like(prev_ref)

    x = x_ref[...]
    h = _rms_rows(x, g_ref[...]).astype(BF16)
    rowid = lax.broadcasted_iota(jnp.int32, (tm, cw), 0)
    acc = x
    for j in range(dff // cw):
        halves = []
        for base in (0, dff):
            cols = slice(base + j * cw, base + (j + 1) * cw)
            p1 = prev_ref[7:8, cols]
            p2 = prev_ref[6:7, cols]

            def prev1(u, p1=p1):
                return jnp.where(rowid == 0, p1, pltpu.roll(u, 1, 0))

            def prev2(u, p1=p1, p2=p2):
                return jnp.where(rowid == 0, p2, jnp.where(rowid == 1, p1, pltpu.roll(u, 2, 0)))

            u, c = _ffn_chunk(h, prev1, prev2, wu_ref, cw_ref, cb_ref, cols)
            prev_ref[:, cols] = u[tm - 8:tm, :]
            halves.append(c)
        a = (jax.nn.silu(halves[0]) * halves[1]).astype(BF16)
        acc = acc + _dot(a, wd_ref[j * cw:(j + 1) * cw, :])
    o_ref[...] = acc
    tail_ref[...] = prev_ref[6:8, :]


def _ffn_prompt(x, g, w_up, conv_w, conv_b, w_down, tm, cw):
    b, t, d = x.shape
    dff = w_down.shape[0]
    const = lambda bi, i: (0, 0)
    rows = pl.BlockSpec((None, tm, d), lambda bi, i: (bi, i, 0))
    return pl.pallas_call(
        functools.partial(_ffn_prompt_kernel, cw=cw),
        grid=(b, t // tm),
        in_specs=[rows, pl.BlockSpec((1, d), const),
                  pl.BlockSpec((d, 2 * dff), const, pipeline_mode=pl.Buffered(1)),
                  pl.BlockSpec((CONV_WIDTH, 2 * dff), const),
                  pl.BlockSpec((1, 2 * dff), const),
                  pl.BlockSpec((dff, d), const, pipeline_mode=pl.Buffered(1))],
        out_specs=(rows, pl.BlockSpec((None, CONV_WIDTH - 1, 2 * dff), lambda bi, i: (bi, 0, 0))),
        out_shape=(jax.ShapeDtypeStruct((b, t, d), F32),
                   jax.ShapeDtypeStruct((b, CONV_WIDTH - 1, 2 * dff), F32)),
        scratch_shapes=[pltpu.VMEM((8, 2 * dff), F32)],
        compiler_params=_cparams(("arbitrary", "arbitrary")),
        name="ffn_prompt",
    )(x, g, w_up, conv_w, conv_b, w_down)


def _ffn_sample_kernel(x_ref, g_ref, st_ref, wu_ref, cw_ref, cb_ref, wd_ref, o_ref, u_ref, *, cw):
    dff = wd_ref.shape[0]
    x = x_ref[...]
    h = _rms_rows(x, g_ref[...]).astype(BF16)
    acc = x
    for j in range(dff // cw):
        halves = []
        for base in (0, dff):
            cols = slice(base + j * cw, base + (j + 1) * cw)
            s0 = st_ref[0, :, cols]
            s1 = st_ref[1, :, cols]
            u, c = _ffn_chunk(h, lambda u, s1=s1: s1, lambda u, s0=s0: s0, wu_ref, cw_ref, cb_ref, cols)
            u_ref[:, cols] = u
            halves.append(c)
        a = (jax.nn.silu(halves[0]) * halves[1]).astype(BF16)
        acc = acc + _dot(a, wd_ref[j * cw:(j + 1) * cw, :])
    o_ref[...] = acc


def _ffn_sample(x, g, state_t, w_up, conv_w, conv_b, w_down, cw):
    r, d = x.shape
    dff = w_down.shape[0]
    return pl.pallas_call(
        functools.partial(_ffn_sample_kernel, cw=cw),
        out_shape=(jax.ShapeDtypeStruct((r, d), F32), jax.ShapeDtypeStruct((r, 2 * dff), F32)),
        compiler_params=_cparams(None),
        name="ffn_sample",
    )(x, g, state_t, w_up, conv_w, conv_b, w_down)


def _decode_kernel(pt_ref, qf_ref, qd_ref, knf_ref, vnf_ref, knd_ref, vnd_ref, gate_ref, slf_ref,
                   lq1, lk1, lq2, lk2, sub_ref, msuf_ref, ones_ref, *rest, pg, lam_init):
    fk = rest[0 * pg:1 * pg]
    fv = rest[1 * pg:2 * pg]
    dk = rest[2 * pg:3 * pg]
    dv = rest[3 * pg:4 * pg]
    lfp = rest[4 * pg:5 * pg]
    fo_ref, do_ref = rest[5 * pg:5 * pg + 2]
    mf_ref, lf_ref, af_ref, md_ref, ld_ref, ad_ref, carry_ref = rest[5 * pg + 2:]
    j = pl.program_id(1)
    nh = FOX_HEADS
    qf = qf_ref[...]
    qd = qd_ref[...]
    width = qf.shape[1]

    @pl.when(j == 0)
    def _():
        mf_ref[...] = jnp.sum(qf * knf_ref[...], axis=1, keepdims=True)
        lf_ref[...] = jnp.ones_like(lf_ref)
        af_ref[...] = jnp.broadcast_to(vnf_ref[...], af_ref.shape)
        md_ref[...] = jnp.sum(qd * knd_ref[...], axis=1, keepdims=True)
        ld_ref[...] = jnp.ones_like(ld_ref)
        ad_ref[...] = vnd_ref[...]
        carry_ref[...] = slf_ref[...] * LOG2E

    def online(scores, pv, m_ref, l_ref, a_ref):
        m_old = m_ref[...]
        smax = scores[0]
        for s in scores[1:]:
            smax = jnp.maximum(smax, s)
        m_new = jnp.maximum(m_old, jnp.max(smax, axis=1, keepdims=True))
        alpha = jnp.exp2(m_old - m_new)
        acc = alpha * a_ref[...]
        psum = None
        for i, s in enumerate(scores):
            p = jnp.exp2(s - m_new)
            psum = p if psum is None else psum + p
            acc = acc + pv(i, p.astype(BF16))
        m_ref[...] = m_new
        l_ref[...] = alpha * l_ref[...] + jnp.sum(psum, axis=1, keepdims=True)
        a_ref[...] = acc

    hi, mid, lo = _split3(jnp.concatenate([lfp[i][...] for i in range(pg)], axis=0) * LOG2E)
    msuf = msuf_ref[...]
    ones = ones_ref[...]
    within = _dot(hi, msuf) + _dot(mid, msuf) + _dot(lo, msuf)
    total = _dot(hi, ones) + _dot(mid, ones) + _dot(lo, ones)
    carry = carry_ref[...]
    qfb = qf.astype(BF16)
    scores = []
    for i in range(pg):
        s = _dot(qfb, fk[i][...].reshape(width, -1).astype(BF16))
        scores.append(s + carry + within[i * nh:(i + 1) * nh, :])
        carry = carry + total[i * nh:(i + 1) * nh, :]
    carry_ref[...] = carry
    online(scores, lambda i, p: _dot_nt(p, fv[i][...].reshape(width, -1).astype(BF16)),
           mf_ref, lf_ref, af_ref)

    qdb = qd.astype(BF16)
    sub = lax.broadcasted_iota(jnp.int32, (2 * DIFF_HEADS, 1), 0)
    npg = dv[0].shape[0] // DIFF_HEADS

    def diff_pv(i, p):
        out = None
        for hd in range(DIFF_HEADS):
            ph = jnp.where(sub // 2 == hd, p, jnp.zeros_like(p))
            r = _dot(ph, dv[i][pl.ds(hd, npg, stride=DIFF_HEADS), :].astype(BF16))
            out = r if out is None else out + r
        return out

    scores = [_dot(qdb, dk[i][...].reshape(width, -1).astype(BF16)) for i in range(pg)]
    online(scores, diff_pv, md_ref, ld_ref, ad_ref)

    @pl.when(j == pl.num_programs(1) - 1)
    def _():
        o = af_ref[...] / lf_ref[...]
        own = (lax.broadcasted_iota(jnp.int32, o.shape, 1) // HEAD_DIM
               == lax.broadcasted_iota(jnp.int32, o.shape, 0))
        fo_ref[...] = jnp.sum(jnp.where(own, o, 0.0), axis=0, keepdims=True) * gate_ref[...]
        od = ad_ref[...] / ld_ref[...]
        lam = _lambda_value(lq1, lk1, lq2, lk2, lam_init)
        signed = jnp.where(sub % 2 == 0, od, -lam * od)
        d = signed + pltpu.roll(signed, 2 * DIFF_HEADS - 1, 0)
        do_ref[...] = _rms_rows(d, sub_ref[...]) * (1.0 - lam_init)


def _decode_attention(page_table, qf, qd, knf, vnf, knd, vnd, gate, slf_b, lam_params, sub, msuf, ones,
                      fox_k, fox_v, diff_k, diff_v, logf, pg, lam_init):
    ns, n_pages = page_table.shape
    ng = n_pages // pg
    page_size = logf.shape[2]
    width = qf.shape[2]
    seq = lambda shape: pl.BlockSpec((None,) + shape, lambda b, j, pt: (b,) + (0,) * len(shape))
    small = lambda shape: pl.BlockSpec(shape, lambda b, j, pt: (0,) * len(shape))

    def page(shape, i):
        return pl.BlockSpec((None,) + shape,
                            lambda b, j, pt: (pt[b, n_pages - 1 - (j * pg + i)],) + (0,) * len(shape))

    nh = FOX_HEADS
    in_specs = [seq((nh, width)), seq((nh, width)), seq((1, width)), seq((1, width)),
                seq((1, width)), seq((nh, 128)), seq((1, width)), seq((nh, page_size))]
    in_specs += [small((1, HEAD_DIM))] * 4 + [small((1, 128))]
    in_specs += [small((page_size, page_size))] * 2
    args = [qf, qd, knf, vnf, knd, vnd, gate, slf_b, *lam_params, sub, msuf, ones]
    for arr in (fox_k, fox_v, diff_k, diff_v, logf):
        for i in range(pg):
            in_specs.append(page(arr.shape[1:], i))
            args.append(arr)
    grid_spec = pltpu.PrefetchScalarGridSpec(
        num_scalar_prefetch=1,
        grid=(ns, ng),
        in_specs=in_specs,
        out_specs=(seq((1, width)), seq((nh, 128))),
        scratch_shapes=[pltpu.VMEM((nh, 1), F32), pltpu.VMEM((nh, 1), F32), pltpu.VMEM((nh, width), F32),
                        pltpu.VMEM((nh, 1), F32), pltpu.VMEM((nh, 1), F32), pltpu.VMEM((nh, 128), F32),
                        pltpu.VMEM((nh, page_size), F32)],
    )
    return pl.pallas_call(
        functools.partial(_decode_kernel, pg=pg, lam_init=lam_init),
        grid_spec=grid_spec,
        out_shape=(jax.ShapeDtypeStruct((ns, 1, width), F32),
                   jax.ShapeDtypeStruct((ns, nh, 128), F32)),
        compiler_params=_cparams(("arbitrary", "arbitrary")),
        name="decode_attention",
    )(page_table, *args)


def _cross_sample_kernel(q_ref, mk_ref, mv_ref, o_ref):
    for i in range(q_ref.shape[0]):
        q = q_ref[i]
        for hd in range(MEM_HEADS):
            k = mk_ref[i, :, hd, :].astype(BF16)
            v = mv_ref[i, :, hd, :].astype(BF16)
            s = _dot_nt(q, k)[hd:hd + 1, :]
            m = jnp.max(s, axis=1, keepdims=True)
            p = jnp.exp(s - m)
            l = jnp.sum(p, axis=1, keepdims=True)
            o_ref[i, hd:hd + 1, :] = _dot(p.astype(BF16), v) / l


def _cross_sample(qn, mem_k, mem_v):
    ns = qn.shape[0]
    m = mem_k.shape[1]
    sb = 1
    qspec = pl.BlockSpec((sb, MEM_HEADS, MEM_HEAD_DIM), lambda b: (b, 0, 0))
    mspec = pl.BlockSpec((sb, m, MEM_HEADS, MEM_HEAD_DIM), lambda b: (b, 0, 0, 0))
    return pl.pallas_call(
        _cross_sample_kernel,
        grid=(ns // sb,),
        in_specs=[qspec, mspec, mspec],
        out_specs=qspec,
        out_shape=jax.ShapeDtypeStruct((ns, MEM_HEADS, MEM_HEAD_DIM), F32),
        compiler_params=_cparams(("arbitrary",)),
        name="cross_sample",
    )(qn, mem_k, mem_v)


def _proj_res_kernel(x_ref, a_ref, w_ref, o_ref):
    o_ref[...] = x_ref[...] + _dot(a_ref[...].astype(BF16), w_ref[...])


def _proj_res(x, a, w):
    return pl.pallas_call(
        _proj_res_kernel,
        out_shape=jax.ShapeDtypeStruct(x.shape, F32),
        compiler_params=_cparams(None),
        name="proj_residual",
    )(x, a, w)


def _rope_tables(pos):
    half = HEAD_DIM // 2
    inv = ROPE_THETA ** (-2.0 * jnp.arange(half, dtype=F32) / HEAD_DIM)
    ang = pos.astype(F32)[:, None] * inv
    cos, sin = jnp.cos(ang), jnp.sin(ang)
    return jnp.tile(cos, (1, 4)), jnp.concatenate([-sin, sin, -sin, sin], axis=1)


def _group_mean_matrix():
    g = np.arange(256) // HEAD_DIM
    return jnp.asarray((g[:, None] == g[None, :]).astype(np.float32) / HEAD_DIM, dtype=BF16)


def _tri_matrix():
    i = np.arange(256)
    return jnp.asarray((i[:, None] <= i[None, :]).astype(np.float32), dtype=BF16)


def _page_matrices(page_size):
    i = np.arange(page_size)
    later = (i[:, None] > i[None, :]).astype(np.float32)
    return jnp.asarray(later, dtype=BF16), jnp.ones((page_size, page_size), BF16)


def _row_tile(r, pref):
    return pref if r % pref == 0 else r


def kernel(x_prompt, x_sample, cache_fox_k, cache_fox_v, cache_fox_logf, cache_diff_k, cache_diff_v, cache_mem_k, cache_mem_v, state_ffn_conv, page_table, mem_prompt, attn_norm_g, w_in, b_forget, fox_q_norm_g, fox_k_norm_g, diff_q_norm_g, diff_k_norm_g, lambda_q1, lambda_k1, lambda_q2, lambda_k2, diff_subln_g, w_out, cross_norm_g, mem_norm_g, w_cq, w_ckv, cross_q_norm_g, cross_k_norm_g, w_co, ffn_norm_g, w_up, conv_w, conv_b, w_down):
    depth = w_in.shape[0]
    assert depth == 1, "one layer per step"
    l = 0
    lam_init = 0.8 - 0.6 * math.exp(-0.3 * l)
    b, t, d = x_prompt.shape
    ns, ts, _ = x_sample.shape
    assert ts == 1
    n_pool, page_size = cache_fox_k.shape[1], cache_fox_k.shape[2]
    n_pages = page_table.shape[1]
    past_len = n_pages * page_size
    dff = w_down.shape[1]
    m_tok = mem_prompt.shape[1]

    sizes = (SEG, SEG, SEG, FOX_HEADS, SEG, SEG, SEG, SEG)
    offs = np.concatenate([[0], np.cumsum(sizes)])
    w = w_in[l]
    seg = lambda i: w[:, offs[i]:offs[i + 1]]
    w_prep = jnp.concatenate(
        [seg(0), seg(1), seg(2), seg(4), seg(5), seg(6), seg(7),
         jnp.pad(seg(3), ((0, 0), (0, FF_PAD - FOX_HEADS)))], axis=1).astype(BF16)
    bf_pad = jnp.pad(b_forget[l], (0, FF_PAD - FOX_HEADS))[None, :]
    qscale = HEAD_DIM ** -0.5 * LOG2E
    fqg, dqg = fox_q_norm_g[l] * qscale, diff_q_norm_g[l] * qscale
    tile8 = lambda g: jnp.tile(g, SEG // HEAD_DIM)
    gains = jnp.stack([tile8(fqg), tile8(fox_k_norm_g[l]), tile8(dqg), tile8(diff_k_norm_g[l])])
    amax = lambda g: jnp.max(jnp.abs(g))
    fox_bound = 1.01 * HEAD_DIM * amax(fqg) * amax(fox_k_norm_g[l])
    diff_bound = 1.01 * HEAD_DIM * amax(dqg) * amax(diff_k_norm_g[l])
    gm = _group_mean_matrix()
    g_attn = attn_norm_g[l][None, :]
    lam_params = (lambda_q1[l][None, :], lambda_k1[l][None, :], lambda_q2[l][None, :], lambda_k2[l][None, :])
    sub_g = diff_subln_g[l][None, :]
    w_out_b = w_out[l].astype(BF16)
    w_cq_b = w_cq[l].astype(BF16)
    w_co_b = w_co[l].astype(BF16)
    w_ckv_b = w_ckv[l].astype(BF16)
    w_up_b = w_up[l].astype(BF16)
    w_down_b = w_down[l].astype(BF16)
    g_cross = cross_norm_g[l][None, :]
    qg = (cross_q_norm_g[l] * MEM_HEAD_DIM ** -0.5)[None, :]
    kg = cross_k_norm_g[l][None, :]
    g_ffn = ffn_norm_g[l][None, :]
    cw_l = conv_w[l]
    cb_l = conv_b[l][None, :]
    ffn_cw = 256 if dff % 256 == 0 else 128

    tm = _row_tile(t, 256)
    cos_p, sin_p = _rope_tables(jnp.arange(t))
    xp2d = x_prompt.reshape(b * t, d)
    (fq, fk_fm, fkb, fv_fm, fvb, gate, dq, dk_fm, dkb, dv, dvb, lf) = _mixer_proj(
        xp2d, g_attn, w_prep, bf_pad, gains, cos_p, sin_p, gm, tm, t // tm, True)
    token_major = lambda a: jnp.moveaxis(a, 1, 2)
    fk, fv, dk = token_major(fk_fm), token_major(fv_fm), token_major(dk_fm)

    lf_rows = jnp.transpose(lf.reshape(b, t, FOX_HEADS), (0, 2, 1)).reshape(b * FOX_HEADS, t)
    c_hk = _cumsum_lanes(lf_rows, _tri_matrix()).reshape(b, FOX_HEADS, t)
    c_rows = c_hk.reshape(b, FOX_HEADS // 2, 2, t)
    c_cols = jnp.pad(jnp.transpose(c_hk, (0, 2, 1)), ((0, 0), (0, 0), (0, 128 - FOX_HEADS)))

    t_blk = _row_tile(t, 1024)
    r3 = lambda a: a.reshape(b, t, SEG)
    fox_o = _prompt_attention("fox", fox_bound, r3(fq), r3(fkb), r3(fvb), (c_rows, c_cols, r3(gate)), t_blk)
    diff_o = _prompt_attention("diff", diff_bound, r3(dq), r3(dkb), r3(dvb), lam_params + (sub_g,), t_blk,
                               lam_init)

    mk, mkb, mv, mvb = _mem_kv(mem_prompt.reshape(b * m_tok, d), mem_norm_g[l][None, :], w_ckv_b, kg,
                               _row_tile(b * m_tok, 256))
    x1, qn = _merge_q(xp2d, fox_o.reshape(b * t, SEG), diff_o.reshape(b * t, SEG),
                      w_out_b, g_cross, w_cq_b, qg, tm)
    x2 = _cross_prompt(x1.reshape(b, t, d), qn.reshape(b, t, d),
                       mkb.reshape(b, m_tok, d), mvb.reshape(b, m_tok, d), w_co_b, tm)
    y_prompt, p_conv = _ffn_prompt(x2, g_ffn, w_up_b, cw_l, cb_l, w_down_b, tm, ffn_cw)

    cos_s, sin_s = _rope_tables(jnp.full((ns,), past_len))
    xs2d = x_sample.reshape(ns, d)
    (sq, sk, _, sv, _, sgate, sdq, sdk, _, sdv, _, slf) = _mixer_proj(
        xs2d, g_attn, w_prep, bf_pad, gains, cos_s, sin_s, gm, ns, 1, False)

    eye = jnp.eye(FOX_HEADS, dtype=F32)
    block_diag = lambda q: jnp.einsum("nhd,hg->nhgd", q.astype(F32).reshape(ns, FOX_HEADS, HEAD_DIM),
                                      eye).reshape(ns, FOX_HEADS, SEG)
    row3 = lambda a: a.reshape(ns, 1, SEG)
    msuf, ones = _page_matrices(page_size)
    fmajor = lambda c: jnp.moveaxis(c.reshape(n_pool, page_size, FOX_HEADS, HEAD_DIM), 1, 3)
    fo_s, do_s = _decode_attention(
        page_table, block_diag(sq), block_diag(sdq), row3(sk), row3(sv), row3(sdk),
        jnp.repeat(sdv.reshape(ns, DIFF_HEADS, 128), 2, axis=1), row3(sgate),
        jnp.broadcast_to(slf[:, :, None], (ns, FOX_HEADS, page_size)),
        lam_params, sub_g, msuf, ones,
        fmajor(cache_fox_k[l]), fmajor(cache_fox_v[l]), fmajor(cache_diff_k[l]),
        cache_diff_v[l].reshape(n_pool, page_size * DIFF_HEADS, 128),
        jnp.moveaxis(cache_fox_logf[l], 1, 2),
        8 if n_pages % 8 == 0 else 1, lam_init)
    do_s = do_s[:, ::2, :]

    xs1, sqn = _merge_q(xs2d, fo_s.reshape(ns, SEG).astype(BF16), do_s.reshape(ns, SEG).astype(BF16),
                        w_out_b, g_cross, w_cq_b, qg, ns)
    co = _cross_sample(sqn.reshape(ns, MEM_HEADS, MEM_HEAD_DIM), cache_mem_k[l], cache_mem_v[l])
    xs2 = _proj_res(xs1, co.reshape(ns, d), w_co_b)
    y_sample, su = _ffn_sample(xs2, g_ffn, jnp.transpose(state_ffn_conv[l], (1, 0, 2)),
                               w_up_b, cw_l, cb_l, w_down_b, ffn_cw)
    s_conv = jnp.stack([state_ffn_conv[l][:, 1, :], su], axis=1)

    st = lambda a, shape: a.reshape((1,) + shape)
    return (
        y_prompt, y_sample.reshape(ns, 1, d),
        st(fk, (b, t, FOX_HEADS, HEAD_DIM)), st(fv, (b, t, FOX_HEADS, HEAD_DIM)),
        st(lf, (b, t, FOX_HEADS)),
        st(dk, (b, t, DIFF_HEADS, 2, HEAD_DIM)), st(dv, (b, t, DIFF_HEADS, 2 * HEAD_DIM)),
        st(mk, (b, m_tok, MEM_HEADS, MEM_HEAD_DIM)), st(mv, (b, m_tok, MEM_HEADS, MEM_HEAD_DIM)),
        st(p_conv, (b, CONV_WIDTH - 1, 2 * dff)),
        st(sk, (ns, 1, FOX_HEADS, HEAD_DIM)), st(sv, (ns, 1, FOX_HEADS, HEAD_DIM)),
        st(slf, (ns, 1, FOX_HEADS)),
        st(sdk, (ns, 1, DIFF_HEADS, 2, HEAD_DIM)), st(sdv, (ns, 1, DIFF_HEADS, 2 * HEAD_DIM)),
        st(s_conv, (ns, CONV_WIDTH - 1, 2 * dff)),
    )
```

```python
import functools
import math

import jax
import jax.numpy as jnp
import numpy as np
from jax import lax
from jax.experimental import pallas as pl
from jax.experimental.pallas import tpu as pltpu

F32 = jnp.float32
BF16 = jnp.bfloat16

HEAD_DIM = 64
FOX_HEADS = 8
DIFF_HEADS = 4
SEG = 512
FF_PAD = 128
MEM_HEADS = 4
MEM_HEAD_DIM = 256
CONV_WIDTH = 3
ROPE_THETA = 10000.0
EPS = 1e-6
NEG = -1e30
LOG2E = math.log2(math.e)
BOUND_LIMIT = 56.0
VMEM_LIMIT = 56 * 1024 * 1024

_NT = (((1,), (1,)), ((), ()))


def _dot(a, b):
    return jnp.dot(a, b, preferred_element_type=F32)


def _dot_nt(a, b):
    return lax.dot_general(a, b, _NT, preferred_element_type=F32)


def _split3(x):
    hi = x.astype(BF16)
    r = x - hi.astype(F32)
    mid = r.astype(BF16)
    lo = (r - mid.astype(F32)).astype(BF16)
    return hi, mid, lo


def _rms_rows(x, g):
    ms = jnp.mean(x * x, axis=-1, keepdims=True)
    return x * lax.rsqrt(ms + EPS) * g


def _cparams(sem):
    return pltpu.CompilerParams(dimension_semantics=sem, vmem_limit_bytes=VMEM_LIMIT)


def _proj_kernel(x_ref, g_ref, w_ref, bf_ref, gains_ref, cos_ref, sin_ref, gm_ref,
                 fq_ref, fk_ref, fkb_ref, fv_ref, fvb_ref, gate_ref,
                 dq_ref, dk_ref, dkb_ref, dv_ref, dvb_ref, lf_ref, *, feature_major):
    def put(ref, val):
        ref[...] = val.T if feature_major else val

    tm = x_ref.shape[0]
    h = _rms_rows(x_ref[...], g_ref[...]).astype(BF16)
    gm = gm_ref[...]

    def seg(i):
        return _dot(h, w_ref[:, i * SEG:(i + 1) * SEG])

    def group_norm(z, gain):
        parts = []
        for c in range(SEG // 256):
            zc = z[:, c * 256:(c + 1) * 256]
            zz = zc * zc
            hi = zz.astype(BF16)
            lo = (zz - hi.astype(F32)).astype(BF16)
            ms = _dot(hi, gm) + _dot(lo, gm)
            parts.append(zc * lax.rsqrt(ms + EPS))
        return jnp.concatenate(parts, axis=1) * gain

    cos = jnp.concatenate([cos_ref[...]] * (SEG // 128), axis=1)
    sin = jnp.concatenate([sin_ref[...]] * (SEG // 128), axis=1)
    first_half = (lax.broadcasted_iota(jnp.int32, (tm, SEG), 1) & (HEAD_DIM // 2)) == 0

    def rope(y):
        partner = jnp.where(first_half,
                            pltpu.roll(y, SEG - HEAD_DIM // 2, 1),
                            pltpu.roll(y, HEAD_DIM // 2, 1))
        return y * cos + partner * sin

    fq_ref[...] = group_norm(seg(0), gains_ref[0:1, :]).astype(BF16)
    fk = group_norm(seg(1), gains_ref[1:2, :])
    put(fk_ref, fk)
    fkb_ref[...] = fk.astype(BF16)
    fv = seg(2)
    put(fv_ref, fv)
    fvb_ref[...] = fv.astype(BF16)
    gate_ref[...] = jax.nn.sigmoid(seg(3))
    dq_ref[...] = rope(group_norm(seg(4), gains_ref[2:3, :])).astype(BF16)
    dk = rope(group_norm(seg(5), gains_ref[3:4, :]))
    put(dk_ref, dk)
    dkb_ref[...] = dk.astype(BF16)
    dv = seg(6)
    dv_ref[...] = dv
    dvb_ref[...] = dv.astype(BF16)
    zf = _dot(h, w_ref[:, 7 * SEG:7 * SEG + FF_PAD]) + bf_ref[...]
    lf = jnp.minimum(zf, 0.0) - jnp.log1p(jnp.exp(-jnp.abs(zf)))
    lf_ref[...] = lf[:, :FOX_HEADS]


def _mixer_proj(x2d, g, w_prep, bf_pad, gains, cos_t, sin_t, gm, tm, n_tab_blocks, feature_major):
    r, d = x2d.shape
    npad = w_prep.shape[1]
    const = lambda i: (0, 0)
    row = lambda i: (i, 0)
    tab = (lambda i: (i % n_tab_blocks, 0)) if n_tab_blocks > 1 else const
    wide_f32 = jax.ShapeDtypeStruct((r, SEG), F32)
    wide_bf = jax.ShapeDtypeStruct((r, SEG), BF16)
    wide_spec = pl.BlockSpec((tm, SEG), row)
    if feature_major:
        kv_f32 = jax.ShapeDtypeStruct((r // (tm * n_tab_blocks), SEG, tm * n_tab_blocks), F32)
        kv_spec = pl.BlockSpec((None, SEG, tm), lambda i: (i // n_tab_blocks, 0, i % n_tab_blocks))
    else:
        kv_f32, kv_spec = wide_f32, wide_spec
    out_shape = (wide_bf, kv_f32, wide_bf, kv_f32, wide_bf, wide_f32,
                 wide_bf, kv_f32, wide_bf, wide_f32, wide_bf,
                 jax.ShapeDtypeStruct((r, FOX_HEADS), F32))
    out_specs = (wide_spec, kv_spec, wide_spec, kv_spec, wide_spec, wide_spec,
                 wide_spec, kv_spec, wide_spec, wide_spec, wide_spec,
                 pl.BlockSpec((tm, FOX_HEADS), row))
    return pl.pallas_call(
        functools.partial(_proj_kernel, feature_major=feature_major),
        grid=(r // tm,),
        in_specs=[
            pl.BlockSpec((tm, d), row),
            pl.BlockSpec((1, d), const),
            pl.BlockSpec((d, npad), const),
            pl.BlockSpec((1, FF_PAD), const),
            pl.BlockSpec((4, SEG), const),
            pl.BlockSpec((tm, 128), tab),
            pl.BlockSpec((tm, 128), tab),
            pl.BlockSpec((256, 256), const),
        ],
        out_specs=out_specs,
        out_shape=out_shape,
        compiler_params=_cparams(("arbitrary",)),
        name="mixer_proj",
    )(x2d, g, w_prep, bf_pad, gains, cos_t, sin_t, gm)


def _cumsum_kernel(x_ref, tri_ref, o_ref):
    rows, t = x_ref.shape
    tri = tri_ref[...]

    def body(c, carry):
        c0 = pl.multiple_of(c * 256, 256)
        hi, mid, lo = _split3(x_ref[:, pl.ds(c0, 256)])
        y = _dot(hi, tri) + _dot(mid, tri) + _dot(lo, tri) + carry
        o_ref[:, pl.ds(c0, 256)] = y * LOG2E
        return y[:, 255:256]

    lax.fori_loop(0, t // 256, body, jnp.zeros((rows, 1), F32))


def _cumsum_lanes(x, tri):
    rows, t = x.shape
    return pl.pallas_call(
        _cumsum_kernel,
        out_shape=jax.ShapeDtypeStruct((rows, t), F32),
        compiler_params=_cparams(None),
        name="logf_cumsum",
    )(x, tri)


def _lambda_value(lq1, lk1, lq2, lk2, lam_init):
    a = jnp.sum(lq1[...] * lk1[...], axis=-1, keepdims=True)
    b = jnp.sum(lq2[...] * lk2[...], axis=-1, keepdims=True)
    return jnp.exp(a) - jnp.exp(b) + lam_init


def _split_maps(q_ref):
    q = q_ref[...]
    lane = lax.broadcasted_iota(jnp.int32, q.shape, 1)
    zero = jnp.zeros_like(q)
    return jnp.where(lane < HEAD_DIM, q, zero), jnp.where(lane >= HEAD_DIM, q, zero), lane


def _attn_online(qi, q_ref, k_ref, v_ref, crow_ref, tq, tk):
    qa, qb, _ = _split_maps(q_ref)

    def update(s, v, m, l, acc):
        m_new = jnp.maximum(m, jnp.max(s, axis=1, keepdims=True))
        alpha = jnp.exp2(m - m_new)
        p = jnp.exp2(s - m_new)
        l = alpha * l + jnp.sum(p, axis=1, keepdims=True)
        acc = alpha * acc + _dot(p.astype(BF16), v)
        return m_new, l, acc

    def step(j, carry, masked):
        ma, la, acca, mb, lb, accb = carry
        k0 = pl.multiple_of(j * tk, tk)
        k = k_ref[pl.ds(k0, tk), :]
        v = v_ref[pl.ds(k0, tk), :]
        sa = _dot_nt(qa, k)
        sb = _dot_nt(qb, k)
        if crow_ref is not None:
            sa = sa - crow_ref[0:1, pl.ds(k0, tk)]
            sb = sb - crow_ref[1:2, pl.ds(k0, tk)]
        if masked:
            keep = (lax.broadcasted_iota(jnp.int32, (tq, tk), 1)
                    <= lax.broadcasted_iota(jnp.int32, (tq, tk), 0))
            sa = jnp.where(keep, sa, NEG)
            sb = jnp.where(keep, sb, NEG)
        ma, la, acca = update(sa, v, ma, la, acca)
        mb, lb, accb = update(sb, v, mb, lb, accb)
        return ma, la, acca, mb, lb, accb

    m0 = jnp.full((tq, 1), NEG, F32)
    l0 = jnp.zeros((tq, 1), F32)
    a0 = jnp.zeros((tq, 128), F32)
    carry = lax.fori_loop(0, qi, functools.partial(step, masked=False), (m0, l0, a0, m0, l0, a0))
    ma, la, acca, mb, lb, accb = step(qi, carry, True)
    return acca / la, accb / lb


def _attn_bounded(qi, q_ref, k_ref, v_ref, crow_ref, rowa, rowb, tq, tk):
    qa, qb, _ = _split_maps(q_ref)
    nchunk = tk // 128
    keep128 = (lax.broadcasted_iota(jnp.int32, (tq, 128), 1)
               <= lax.broadcasted_iota(jnp.int32, (tq, 128), 0))
    row_id = lax.broadcasted_iota(jnp.int32, (tq, 128), 0)

    def one_map(s, row, col, v, l, acc, masked):
        ps = []
        for c in range(nchunk):
            t = s[:, c * 128:(c + 1) * 128] + row
            if col is not None:
                t = t + col[:, c * 128:(c + 1) * 128]
            if masked:
                t = jnp.where(keep128 if c == 0 else
                              (lax.broadcasted_iota(jnp.int32, (tq, 128), 1) + c * 128 <= row_id), t, NEG)
            p = jnp.exp2(t)
            l = l + p
            ps.append(p.astype(BF16))
        acc = acc + _dot(jnp.concatenate(ps, axis=1), v)
        return l, acc

    def step(j, carry, masked):
        la, acca, lb, accb = carry
        k0 = pl.multiple_of(j * tk, tk)
        k = k_ref[pl.ds(k0, tk), :]
        v = v_ref[pl.ds(k0, tk), :]
        cola = colb = None
        if crow_ref is not None:
            cola = -crow_ref[0:1, pl.ds(k0, tk)]
            colb = -crow_ref[1:2, pl.ds(k0, tk)]
        la, acca = one_map(_dot_nt(qa, k), rowa, cola, v, la, acca, masked)
        lb, accb = one_map(_dot_nt(qb, k), rowb, colb, v, lb, accb, masked)
        return la, acca, lb, accb

    z = jnp.zeros((tq, 128), F32)
    carry = lax.fori_loop(0, qi, functools.partial(step, masked=False), (z, z, z, z))
    la, acca, lb, accb = step(qi, carry, True)
    return (acca / jnp.sum(la, axis=1, keepdims=True), accb / jnp.sum(lb, axis=1, keepdims=True))


def _fox_attn_kernel(fast_ref, bnd_ref, q_ref, k_ref, v_ref, crow_ref, ccol_ref, gate_ref, o_ref, *, tq, tk):
    hp = pl.program_id(1)
    qi = pl.program_id(2)
    fast = fast_ref[0, 0]
    lane = lax.broadcasted_iota(jnp.int32, (tq, 128), 1)

    def finish(oa, ob):
        o = jnp.where(lane < HEAD_DIM, oa, ob) * gate_ref[...]
        o_ref[...] = o.astype(o_ref.dtype)

    @pl.when(fast == 1)
    def _():
        hi, mid, lo = _split3(ccol_ref[...])
        src = lax.broadcasted_iota(jnp.int32, (128, 128), 0)
        bound = bnd_ref[...]

        def pick(h):
            sel = jnp.where(src == h, 1.0, 0.0).astype(BF16)
            return _dot(hi, sel) + _dot(mid, sel) + _dot(lo, sel) - bound

        finish(*_attn_bounded(qi, q_ref, k_ref, v_ref, crow_ref, pick(2 * hp), pick(2 * hp + 1), tq, tk))

    @pl.when(fast != 1)
    def _():
        finish(*_attn_online(qi, q_ref, k_ref, v_ref, crow_ref, tq, tk))


def _diff_attn_kernel(fast_ref, bnd_ref, q_ref, k_ref, v_ref, lq1, lk1, lq2, lk2, sub_ref, o_ref, *,
                      tq, tk, lam_init):
    qi = pl.program_id(2)
    fast = fast_ref[0, 0]

    def finish(oa, ob):
        lam = _lambda_value(lq1, lk1, lq2, lk2, lam_init)
        d = oa - lam * ob
        o = _rms_rows(d, sub_ref[...]) * (1.0 - lam_init)
        o_ref[...] = o.astype(o_ref.dtype)

    @pl.when(fast == 1)
    def _():
        row = -bnd_ref[...]
        finish(*_attn_bounded(qi, q_ref, k_ref, v_ref, None, row, row, tq, tk))

    @pl.when(fast != 1)
    def _():
        finish(*_attn_online(qi, q_ref, k_ref, v_ref, None, tq, tk))


def _prompt_attention(kind, bound, q, k, v, extra, t_blk, lam_init=None):
    b, t, _ = q.shape
    nblk = SEG // 128
    tq = tk = t_blk
    qspec = pl.BlockSpec((None, tq, 128), lambda bi, hi, qi: (bi, qi, hi))
    kvspec = pl.BlockSpec((None, t, 128), lambda bi, hi, qi: (bi, 0, hi))
    small = lambda shape: pl.BlockSpec(shape, lambda bi, hi, qi: (0,) * len(shape))
    smem = pl.BlockSpec(memory_space=pltpu.SMEM)
    fast = (bound <= BOUND_LIMIT).astype(jnp.int32).reshape(1, 1)
    bound_row = jnp.broadcast_to(bound.astype(F32).reshape(1, 1), (1, 128))
    if kind == "fox":
        crow, ccol, gate = extra
        body = functools.partial(_fox_attn_kernel, tq=tq, tk=tk)
        in_specs = [smem, small((1, 128)), qspec, kvspec, kvspec,
                    pl.BlockSpec((None, None, 2, t), lambda bi, hi, qi: (bi, hi, 0, 0)),
                    pl.BlockSpec((None, tq, 128), lambda bi, hi, qi: (bi, qi, 0)),
                    qspec]
        args = (fast, bound_row, q, k, v, crow, ccol, gate)
    else:
        lq1, lk1, lq2, lk2, sub = extra
        body = functools.partial(_diff_attn_kernel, tq=tq, tk=tk, lam_init=lam_init)
        in_specs = [smem, small((1, 128)), qspec, kvspec, kvspec] + [small((1, HEAD_DIM))] * 4 + [small((1, 128))]
        args = (fast, bound_row, q, k, v, lq1, lk1, lq2, lk2, sub)
    return pl.pallas_call(
        body,
        grid=(b, nblk, t // tq),
        in_specs=in_specs,
        out_specs=qspec,
        out_shape=jax.ShapeDtypeStruct((b, t, SEG), BF16),
        compiler_params=_cparams(("arbitrary", "arbitrary", "arbitrary")),
        name=kind + "_attention",
    )(*args)


def _merge_q_kernel(x_ref, fox_ref, diff_ref, wo_ref, g_ref, wq_ref, qg_ref, x1_ref, qn_ref):
    x1 = x_ref[...] + _dot(fox_ref[...], wo_ref[0:SEG, :]) + _dot(diff_ref[...], wo_ref[SEG:2 * SEG, :])
    x1_ref[...] = x1
    h = _rms_rows(x1, g_ref[...]).astype(BF16)
    for hd in range(MEM_HEADS):
        sl = slice(hd * MEM_HEAD_DIM, (hd + 1) * MEM_HEAD_DIM)
        qh = _dot(h, wq_ref[:, sl])
        qn_ref[:, sl] = _rms_rows(qh, qg_ref[...]).astype(BF16)


def _merge_q(x2d, fox, diff, w_out, g, w_cq, qg, tm):
    r, d = x2d.shape
    const = lambda i: (0, 0)
    row = lambda i: (i, 0)
    return pl.pallas_call(
        _merge_q_kernel,
        grid=(r // tm,),
        in_specs=[
            pl.BlockSpec((tm, d), row),
            pl.BlockSpec((tm, SEG), row),
            pl.BlockSpec((tm, SEG), row),
            pl.BlockSpec((2 * SEG, d), const),
            pl.BlockSpec((1, d), const),
            pl.BlockSpec((d, d), const),
            pl.BlockSpec((1, MEM_HEAD_DIM), const),
        ],
        out_specs=(pl.BlockSpec((tm, d), row), pl.BlockSpec((tm, d), row)),
        out_shape=(jax.ShapeDtypeStruct((r, d), F32), jax.ShapeDtypeStruct((r, d), BF16)),
        compiler_params=_cparams(("arbitrary",)),
        name="merge_q",
    )(x2d, fox, diff, w_out, g, w_cq, qg)


def _mem_kv_kernel(m_ref, g_ref, w_ref, kg_ref, k_ref, kb_ref, v_ref, vb_ref):
    d = m_ref.shape[1]
    h = _rms_rows(m_ref[...], g_ref[...]).astype(BF16)
    for hd in range(MEM_HEADS):
        sl = slice(hd * MEM_HEAD_DIM, (hd + 1) * MEM_HEAD_DIM)
        kh = _rms_rows(_dot(h, w_ref[:, sl]), kg_ref[...])
        k_ref[:, sl] = kh
        kb_ref[:, sl] = kh.astype(BF16)
    v = _dot(h, w_ref[:, d:2 * d])
    v_ref[...] = v
    vb_ref[...] = v.astype(BF16)


def _mem_kv(mem2d, g, w_ckv, kg, tm):
    r, d = mem2d.shape
    const = lambda i: (0, 0)
    row = lambda i: (i, 0)
    f = jax.ShapeDtypeStruct((r, d), F32)
    h = jax.ShapeDtypeStruct((r, d), BF16)
    spec = pl.BlockSpec((tm, d), row)
    return pl.pallas_call(
        _mem_kv_kernel,
        grid=(r // tm,),
        in_specs=[spec, pl.BlockSpec((1, d), const), pl.BlockSpec((d, 2 * d), const),
                  pl.BlockSpec((1, MEM_HEAD_DIM), const)],
        out_specs=(spec, spec, spec, spec),
        out_shape=(f, h, f, h),
        compiler_params=_cparams(("arbitrary",)),
        name="mem_kv",
    )(mem2d, g, w_ckv, kg)


def _cross_kernel(x1_ref, qn_ref, mk_ref, mv_ref, wo_ref, o_ref):
    acc = x1_ref[...]
    for hd in range(MEM_HEADS):
        sl = slice(hd * MEM_HEAD_DIM, (hd + 1) * MEM_HEAD_DIM)
        s = _dot_nt(qn_ref[:, sl], mk_ref[:, sl])
        m = jnp.max(s, axis=1, keepdims=True)
        p = jnp.exp(s - m)
        l = jnp.sum(p, axis=1, keepdims=True)
        oh = _dot(p.astype(BF16), mv_ref[:, sl]) / l
        acc = acc + _dot(oh.astype(BF16), wo_ref[sl, :])
    o_ref[...] = acc


def _cross_prompt(x1, qn, mkb, mvb, w_co, tm):
    b, t, d = x1.shape
    m = mkb.shape[1]
    rows = pl.BlockSpec((None, tm, d), lambda bi, i: (bi, i, 0))
    mem = pl.BlockSpec((None, m, d), lambda bi, i: (bi, 0, 0))
    return pl.pallas_call(
        _cross_kernel,
        grid=(b, t // tm),
        in_specs=[rows, rows, mem, mem, pl.BlockSpec((d, d), lambda bi, i: (0, 0))],
        out_specs=rows,
        out_shape=jax.ShapeDtypeStruct((b, t, d), F32),
        compiler_params=_cparams(("arbitrary", "arbitrary")),
        name="cross_prompt",
    )(x1, qn, mkb, mvb, w_co)


def _ffn_chunk(h, u_prev1, u_prev2, wu_ref, cw_ref, cb_ref, cols):
    u = _dot(h, wu_ref[:, cols])
    c = (cb_ref[:, cols] + cw_ref[0:1, cols] * u_prev2(u) + cw_ref[1:2, cols] * u_prev1(u)
         + cw_ref[2:3, cols] * u)
    return u, c


def _ffn_prompt_kernel(x_ref, g_ref, wu_ref, cw_ref, cb_ref, wd_ref, o_ref, tail_ref, prev_ref, *, cw):
    tm = x_ref.shape[0]
    dff = wd_ref.shape[0]
    ti = pl.program_id(1)

    @pl.when(ti == 0)
    def _():
        prev_ref[...] = jnp.zeros_like(prev_ref)

    x = x_ref[...]
    h = _rms_rows(x, g_ref[...]).astype(BF16)
    rowid = lax.broadcasted_iota(jnp.int32, (tm, cw), 0)
    acc = x
    for j in range(dff // cw):
        halves = []
        for base in (0, dff):
            cols = slice(base + j * cw, base + (j + 1) * cw)
            p1 = prev_ref[7:8, cols]
            p2 = prev_ref[6:7, cols]

            def prev1(u, p1=p1):
                return jnp.where(rowid == 0, p1, pltpu.roll(u, 1, 0))

            def prev2(u, p1=p1, p2=p2):
                return jnp.where(rowid == 0, p2, jnp.where(rowid == 1, p1, pltpu.roll(u, 2, 0)))

            u, c = _ffn_chunk(h, prev1, prev2, wu_ref, cw_ref, cb_ref, cols)
            prev_ref[:, cols] = u[tm - 8:tm, :]
            halves.append(c)
        a = (jax.nn.silu(halves[0]) * halves[1]).astype(BF16)
        acc = acc + _dot(a, wd_ref[j * cw:(j + 1) * cw, :])
    o_ref[...] = acc
    tail_ref[...] = prev_ref[6:8, :]


def _ffn_prompt(x, g, w_up, conv_w, conv_b, w_down, tm, cw):
    b, t, d = x.shape
    dff = w_down.shape[0]
    const = lambda bi, i: (0, 0)
    rows = pl.BlockSpec((None, tm, d), lambda bi, i: (bi, i, 0))
    return pl.pallas_call(
        functools.partial(_ffn_prompt_kernel, cw=cw),
        grid=(b, t // tm),
        in_specs=[rows, pl.BlockSpec((1, d), const),
                  pl.BlockSpec((d, 2 * dff), const, pipeline_mode=pl.Buffered(1)),
                  pl.BlockSpec((CONV_WIDTH, 2 * dff), const),
                  pl.BlockSpec((1, 2 * dff), const),
                  pl.BlockSpec((dff, d), const, pipeline_mode=pl.Buffered(1))],
        out_specs=(rows, pl.BlockSpec((None, CONV_WIDTH - 1, 2 * dff), lambda bi, i: (bi, 0, 0))),
        out_shape=(jax.ShapeDtypeStruct((b, t, d), F32),
                   jax.ShapeDtypeStruct((b, CONV_WIDTH - 1, 2 * dff), F32)),
        scratch_shapes=[pltpu.VMEM((8, 2 * dff), F32)],
        compiler_params=_cparams(("arbitrary", "arbitrary")),
        name="ffn_prompt",
    )(x, g, w_up, conv_w, conv_b, w_down)


def _ffn_sample_kernel(x_ref, g_ref, st_ref, wu_ref, cw_ref, cb_ref, wd_ref, o_ref, u_ref, *, cw):
    dff = wd_ref.shape[0]
    x = x_ref[...]
    h = _rms_rows(x, g_ref[...]).astype(BF16)
    acc = x
    for j in range(dff // cw):
        halves = []
        for base in (0, dff):
            cols = slice(base + j * cw, base + (j + 1) * cw)
            s0 = st_ref[0, :, cols]
            s1 = st_ref[1, :, cols]
            u, c = _ffn_chunk(h, lambda u, s1=s1: s1, lambda u, s0=s0: s0, wu_ref, cw_ref, cb_ref, cols)
            u_ref[:, cols] = u
            halves.append(c)
        a = (jax.nn.silu(halves[0]) * halves[1]).astype(BF16)
        acc = acc + _dot(a, wd_ref[j * cw:(j + 1) * cw, :])
    o_ref[...] = acc


def _ffn_sample(x, g, state_t, w_up, conv_w, conv_b, w_down, cw):
    r, d = x.shape
    dff = w_down.shape[0]
    return pl.pallas_call(
        functools.partial(_ffn_sample_kernel, cw=cw),
        out_shape=(jax.ShapeDtypeStruct((r, d), F32), jax.ShapeDtypeStruct((r, 2 * dff), F32)),
        compiler_params=_cparams(None),
        name="ffn_sample",
    )(x, g, state_t, w_up, conv_w, conv_b, w_down)


def _decode_kernel(pt_ref, qf_ref, qd_ref, knf_ref, vnf_ref, knd_ref, vnd_ref, gate_ref, slf_ref,
                   lq1, lk1, lq2, lk2, sub_ref, msuf_ref, ones_ref, *rest, pg, lam_init):
    fk = rest[0 * pg:1 * pg]
    fv = rest[1 * pg:2 * pg]
    dk = rest[2 * pg:3 * pg]
    dv = rest[3 * pg:4 * pg]
    lfp = rest[4 * pg:5 * pg]
    fo_ref, do_ref = rest[5 * pg:5 * pg + 2]
    mf_ref, lf_ref, af_ref, md_ref, ld_ref, ad_ref, carry_ref = rest[5 * pg + 2:]
    j = pl.program_id(1)
    nh = FOX_HEADS
    qf = qf_ref[...]
    qd = qd_ref[...]
    width = qf.shape[1]

    @pl.when(j == 0)
    def _():
        mf_ref[...] = jnp.sum(qf * knf_ref[...], axis=1, keepdims=True)
        lf_ref[...] = jnp.ones_like(lf_ref)
        af_ref[...] = jnp.broadcast_to(vnf_ref[...], af_ref.shape)
        md_ref[...] = jnp.sum(qd * knd_ref[...], axis=1, keepdims=True)
        ld_ref[...] = jnp.ones_like(ld_ref)
        ad_ref[...] = vnd_ref[...]
        carry_ref[...] = slf_ref[...] * LOG2E

    def online(scores, pv, m_ref, l_ref, a_ref):
        m_old = m_ref[...]
        smax = scores[0]
        for s in scores[1:]:
            smax = jnp.maximum(smax, s)
        m_new = jnp.maximum(m_old, jnp.max(smax, axis=1, keepdims=True))
        alpha = jnp.exp2(m_old - m_new)
        acc = alpha * a_ref[...]
        psum = None
        for i, s in enumerate(scores):
            p = jnp.exp2(s - m_new)
            psum = p if psum is None else psum + p
            acc = acc + pv(i, p.astype(BF16))
        m_ref[...] = m_new
        l_ref[...] = alpha * l_ref[...] + jnp.sum(psum, axis=1, keepdims=True)
        a_ref[...] = acc

    hi, mid, lo = _split3(jnp.concatenate([lfp[i][...] for i in range(pg)], axis=0) * LOG2E)
    msuf = msuf_ref[...]
    ones = ones_ref[...]
    within = _dot(hi, msuf) + _dot(mid, msuf) + _dot(lo, msuf)
    total = _dot(hi, ones) + _dot(mid, ones) + _dot(lo, ones)
    carry = carry_ref[...]
    qfb = qf.astype(BF16)
    scores = []
    for i in range(pg):
        s = _dot(qfb, fk[i][...].reshape(width, -1).astype(BF16))
        scores.append(s + carry + within[i * nh:(i + 1) * nh, :])
        carry = carry + total[i * nh:(i + 1) * nh, :]
    carry_ref[...] = carry
    online(scores, lambda i, p: _dot_nt(p, fv[i][...].reshape(width, -1).astype(BF16)),
           mf_ref, lf_ref, af_ref)

    qdb = qd.astype(BF16)
    sub = lax.broadcasted_iota(jnp.int32, (2 * DIFF_HEADS, 1), 0)
    npg = dv[0].shape[0] // DIFF_HEADS

    def diff_pv(i, p):
        out = None
        for hd in range(DIFF_HEADS):
            ph = jnp.where(sub // 2 == hd, p, jnp.zeros_like(p))
            r = _dot(ph, dv[i][pl.ds(hd, npg, stride=DIFF_HEADS), :].astype(BF16))
            out = r if out is None else out + r
        return out

    scores = [_dot(qdb, dk[i][...].reshape(width, -1).astype(BF16)) for i in range(pg)]
    online(scores, diff_pv, md_ref, ld_ref, ad_ref)

    @pl.when(j == pl.num_programs(1) - 1)
    def _():
        o = af_ref[...] / lf_ref[...]
        own = (lax.broadcasted_iota(jnp.int32, o.shape, 1) // HEAD_DIM
               == lax.broadcasted_iota(jnp.int32, o.shape, 0))
        fo_ref[...] = jnp.sum(jnp.where(own, o, 0.0), axis=0, keepdims=True) * gate_ref[...]
        od = ad_ref[...] / ld_ref[...]
        lam = _lambda_value(lq1, lk1, lq2, lk2, lam_init)
        signed = jnp.where(sub % 2 == 0, od, -lam * od)
        d = signed + pltpu.roll(signed, 2 * DIFF_HEADS - 1, 0)
        do_ref[...] = _rms_rows(d, sub_ref[...]) * (1.0 - lam_init)


def _decode_attention(page_table, qf, qd, knf, vnf, knd, vnd, gate, slf_b, lam_params, sub, msuf, ones,
                      fox_k, fox_v, diff_k, diff_v, logf, pg, lam_init):
    ns, n_pages = page_table.shape
    ng = n_pages // pg
    page_size = logf.shape[2]
    width = qf.shape[2]
    seq = lambda shape: pl.BlockSpec((None,) + shape, lambda b, j, pt: (b,) + (0,) * len(shape))
    small = lambda shape: pl.BlockSpec(shape, lambda b, j, pt: (0,) * len(shape))

    def page(shape, i):
        return pl.BlockSpec((None,) + shape,
                            lambda b, j, pt: (pt[b, n_pages - 1 - (j * pg + i)],) + (0,) * len(shape))

    nh = FOX_HEADS
    in_specs = [seq((nh, width)), seq((nh, width)), seq((1, width)), seq((1, width)),
                seq((1, width)), seq((nh, 128)), seq((1, width)), seq((nh, page_size))]
    in_specs += [small((1, HEAD_DIM))] * 4 + [small((1, 128))]
    in_specs += [small((page_size, page_size))] * 2
    args = [qf, qd, knf, vnf, knd, vnd, gate, slf_b, *lam_params, sub, msuf, ones]
    for arr in (fox_k, fox_v, diff_k, diff_v, logf):
        for i in range(pg):
            in_specs.append(page(arr.shape[1:], i))
            args.append(arr)
    grid_spec = pltpu.PrefetchScalarGridSpec(
        num_scalar_prefetch=1,
        grid=(ns, ng),
        in_specs=in_specs,
        out_specs=(seq((1, width)), seq((nh, 128))),
        scratch_shapes=[pltpu.VMEM((nh, 1), F32), pltpu.VMEM((nh, 1), F32), pltpu.VMEM((nh, width), F32),
                        pltpu.VMEM((nh, 1), F32), pltpu.VMEM((nh, 1), F32), pltpu.VMEM((nh, 128), F32),
                        pltpu.VMEM((nh, page_size), F32)],
    )
    return pl.pallas_call(
        functools.partial(_decode_kernel, pg=pg, lam_init=lam_init),
        grid_spec=grid_spec,
        out_shape=(jax.ShapeDtypeStruct((ns, 1, width), F32),
                   jax.ShapeDtypeStruct((ns, nh, 128), F32)),
        compiler_params=_cparams(("arbitrary", "arbitrary")),
        name="decode_attention",
    )(page_table, *args)


def _cross_sample_kernel(q_ref, mk_ref, mv_ref, o_ref):
    for i in range(q_ref.shape[0]):
        q = q_ref[i]
        for hd in range(MEM_HEADS):
            k = mk_ref[i, :, hd, :].astype(BF16)
            v = mv_ref[i, :, hd, :].astype(BF16)
            s = _dot_nt(q, k)[hd:hd + 1, :]
            m = jnp.max(s, axis=1, keepdims=True)
            p = jnp.exp(s - m)
            l = jnp.sum(p, axis=1, keepdims=True)
            o_ref[i, hd:hd + 1, :] = _dot(p.astype(BF16), v) / l


def _cross_sample(qn, mem_k, mem_v):
    ns = qn.shape[0]
    m = mem_k.shape[1]
    sb = 1
    qspec = pl.BlockSpec((sb, MEM_HEADS, MEM_HEAD_DIM), lambda b: (b, 0, 0))
    mspec = pl.BlockSpec((sb, m, MEM_HEADS, MEM_HEAD_DIM), lambda b: (b, 0, 0, 0))
    return pl.pallas_call(
        _cross_sample_kernel,
        grid=(ns // sb,),
        in_specs=[qspec, mspec, mspec],
        out_specs=qspec,
        out_shape=jax.ShapeDtypeStruct((ns, MEM_HEADS, MEM_HEAD_DIM), F32),
        compiler_params=_cparams(("arbitrary",)),
        name="cross_sample",
    )(qn, mem_k, mem_v)


def _proj_res_kernel(x_ref, a_ref, w_ref, o_ref):
    o_ref[...] = x_ref[...] + _dot(a_ref[...].astype(BF16), w_ref[...])


def _proj_res(x, a, w):
    return pl.pallas_call(
        _proj_res_kernel,
        out_shape=jax.ShapeDtypeStruct(x.shape, F32),
        compiler_params=_cparams(None),
        name="proj_residual",
    )(x, a, w)


def _rope_tables(pos):
    half = HEAD_DIM // 2
    inv = ROPE_THETA ** (-2.0 * jnp.arange(half, dtype=F32) / HEAD_DIM)
    ang = pos.astype(F32)[:, None] * inv
    cos, sin = jnp.cos(ang), jnp.sin(ang)
    return jnp.tile(cos, (1, 4)), jnp.concatenate([-sin, sin, -sin, sin], axis=1)


def _group_mean_matrix():
    g = np.arange(256) // HEAD_DIM
    return jnp.asarray((g[:, None] == g[None, :]).astype(np.float32) / HEAD_DIM, dtype=BF16)


def _tri_matrix():
    i = np.arange(256)
    return jnp.asarray((i[:, None] <= i[None, :]).astype(np.float32), dtype=BF16)


def _page_matrices(page_size):
    i = np.arange(page_size)
    later = (i[:, None] > i[None, :]).astype(np.float32)
    return jnp.asarray(later, dtype=BF16), jnp.ones((page_size, page_size), BF16)


def _row_tile(r, pref):
    return pref if r % pref == 0 else r


def kernel(x_prompt, x_sample, cache_fox_k, cache_fox_v, cache_fox_logf, cache_diff_k, cache_diff_v, cache_mem_k, cache_mem_v, state_ffn_conv, page_table, mem_prompt, attn_norm_g, w_in, b_forget, fox_q_norm_g, fox_k_norm_g, diff_q_norm_g, diff_k_norm_g, lambda_q1, lambda_k1, lambda_q2, lambda_k2, diff_subln_g, w_out, cross_norm_g, mem_norm_g, w_cq, w_ckv, cross_q_norm_g, cross_k_norm_g, w_co, ffn_norm_g, w_up, conv_w, conv_b, w_down):
    depth = w_in.shape[0]
    assert depth == 1, "one layer per step"
    l = 0
    lam_init = 0.8 - 0.6 * math.exp(-0.3 * l)
    b, t, d = x_prompt.shape
    ns, ts, _ = x_sample.shape
    assert ts == 1
    n_pool, page_size = cache_fox_k.shape[1], cache_fox_k.shape[2]
    n_pages = page_table.shape[1]
    past_len = n_pages * page_size
    dff = w_down.shape[1]
    m_tok = mem_prompt.shape[1]

    sizes = (SEG, SEG, SEG, FOX_HEADS, SEG, SEG, SEG, SEG)
    offs = np.concatenate([[0], np.cumsum(sizes)])
    w = w_in[l]
    seg = lambda i: w[:, offs[i]:offs[i + 1]]
    w_prep = jnp.concatenate(
        [seg(0), seg(1), seg(2), seg(4), seg(5), seg(6), seg(7),
         jnp.pad(seg(3), ((0, 0), (0, FF_PAD - FOX_HEADS)))], axis=1).astype(BF16)
    bf_pad = jnp.pad(b_forget[l], (0, FF_PAD - FOX_HEADS))[None, :]
    qscale = HEAD_DIM ** -0.5 * LOG2E
    fqg, dqg = fox_q_norm_g[l] * qscale, diff_q_norm_g[l] * qscale
    tile8 = lambda g: jnp.tile(g, SEG // HEAD_DIM)
    gains = jnp.stack([tile8(fqg), tile8(fox_k_norm_g[l]), tile8(dqg), tile8(diff_k_norm_g[l])])
    amax = lambda g: jnp.max(jnp.abs(g))
    fox_bound = 1.01 * HEAD_DIM * amax(fqg) * amax(fox_k_norm_g[l])
    diff_bound = 1.01 * HEAD_DIM * amax(dqg) * amax(diff_k_norm_g[l])
    gm = _group_mean_matrix()
    g_attn = attn_norm_g[l][None, :]
    lam_params = (lambda_q1[l][None, :], lambda_k1[l][None, :], lambda_q2[l][None, :], lambda_k2[l][None, :])
    sub_g = diff_subln_g[l][None, :]
    w_out_b = w_out[l].astype(BF16)
    w_cq_b = w_cq[l].astype(BF16)
    w_co_b = w_co[l].astype(BF16)
    w_ckv_b = w_ckv[l].astype(BF16)
    w_up_b = w_up[l].astype(BF16)
    w_down_b = w_down[l].astype(BF16)
    g_cross = cross_norm_g[l][None, :]
    qg = (cross_q_norm_g[l] * MEM_HEAD_DIM ** -0.5)[None, :]
    kg = cross_k_norm_g[l][None, :]
    g_ffn = ffn_norm_g[l][None, :]
    cw_l = conv_w[l]
    cb_l = conv_b[l][None, :]
    ffn_cw = next(c for c in (1408, 256, 128) if dff % c == 0)

    tm = _row_tile(t, 256)
    cos_p, sin_p = _rope_tables(jnp.arange(t))
    xp2d = x_prompt.reshape(b * t, d)
    (fq, fk_fm, fkb, fv_fm, fvb, gate, dq, dk_fm, dkb, dv, dvb, lf) = _mixer_proj(
        xp2d, g_attn, w_prep, bf_pad, gains, cos_p, sin_p, gm, tm, t // tm, True)
    token_major = lambda a: jnp.moveaxis(a, 1, 2)
    fk, fv, dk = token_major(fk_fm), token_major(fv_fm), token_major(dk_fm)

    lf_rows = jnp.transpose(lf.reshape(b, t, FOX_HEADS), (0, 2, 1)).reshape(b * FOX_HEADS, t)
    c_hk = _cumsum_lanes(lf_rows, _tri_matrix()).reshape(b, FOX_HEADS, t)
    c_rows = c_hk.reshape(b, FOX_HEADS // 2, 2, t)
    c_cols = jnp.pad(jnp.transpose(c_hk, (0, 2, 1)), ((0, 0), (0, 0), (0, 128 - FOX_HEADS)))

    t_blk = _row_tile(t, 1024)
    r3 = lambda a: a.reshape(b, t, SEG)
    fox_o = _prompt_attention("fox", fox_bound, r3(fq), r3(fkb), r3(fvb), (c_rows, c_cols, r3(gate)), t_blk)
    diff_o = _prompt_attention("diff", diff_bound, r3(dq), r3(dkb), r3(dvb), lam_params + (sub_g,), t_blk,
                               lam_init)

    mk, mkb, mv, mvb = _mem_kv(mem_prompt.reshape(b * m_tok, d), mem_norm_g[l][None, :], w_ckv_b, kg,
                               _row_tile(b * m_tok, 256))
    x1, qn = _merge_q(xp2d, fox_o.reshape(b * t, SEG), diff_o.reshape(b * t, SEG),
                      w_out_b, g_cross, w_cq_b, qg, tm)
    x2 = _cross_prompt(x1.reshape(b, t, d), qn.reshape(b, t, d),
                       mkb.reshape(b, m_tok, d), mvb.reshape(b, m_tok, d), w_co_b, tm)
    y_prompt, p_conv = _ffn_prompt(x2, g_ffn, w_up_b, cw_l, cb_l, w_down_b, tm, ffn_cw)

    cos_s, sin_s = _rope_tables(jnp.full((ns,), past_len))
    xs2d = x_sample.reshape(ns, d)
    (sq, sk, _, sv, _, sgate, sdq, sdk, _, sdv, _, slf) = _mixer_proj(
        xs2d, g_attn, w_prep, bf_pad, gains, cos_s, sin_s, gm, ns, 1, False)

    eye = jnp.eye(FOX_HEADS, dtype=F32)
    block_diag = lambda q: jnp.einsum("nhd,hg->nhgd", q.astype(F32).reshape(ns, FOX_HEADS, HEAD_DIM),
                                      eye).reshape(ns, FOX_HEADS, SEG)
    row3 = lambda a: a.reshape(ns, 1, SEG)
    msuf, ones = _page_matrices(page_size)
    fmajor = lambda c: jnp.moveaxis(c.reshape(n_pool, page_size, FOX_HEADS, HEAD_DIM), 1, 3)
    fo_s, do_s = _decode_attention(
        page_table, block_diag(sq), block_diag(sdq), row3(sk), row3(sv), row3(sdk),
        jnp.repeat(sdv.reshape(ns, DIFF_HEADS, 128), 2, axis=1), row3(sgate),
        jnp.broadcast_to(slf[:, :, None], (ns, FOX_HEADS, page_size)),
        lam_params, sub_g, msuf, ones,
        fmajor(cache_fox_k[l]), fmajor(cache_fox_v[l]), fmajor(cache_diff_k[l]),
        cache_diff_v[l].reshape(n_pool, page_size * DIFF_HEADS, 128),
        jnp.moveaxis(cache_fox_logf[l], 1, 2),
        8 if n_pages % 8 == 0 else 1, lam_init)
    do_s = do_s[:, ::2, :]

    xs1, sqn = _merge_q(xs2d, fo_s.reshape(ns, SEG).astype(BF16), do_s.reshape(ns, SEG).astype(BF16),
                        w_out_b, g_cross, w_cq_b, qg, ns)
    co = _cross_sample(sqn.reshape(ns, MEM_HEADS, MEM_HEAD_DIM), cache_mem_k[l], cache_mem_v[l])
    xs2 = _proj_res(xs1, co.reshape(ns, d), w_co_b)
    y_sample, su = _ffn_sample(xs2, g_ffn, jnp.transpose(state_ffn_conv[l], (1, 0, 2)),
                               w_up_b, cw_l, cb_l, w_down_b, ffn_cw)
    s_conv = jnp.stack([state_ffn_conv[l][:, 1, :], su], axis=1)

    st = lambda a, shape: a.reshape((1,) + shape)
    return (
        y_prompt, y_sample.reshape(ns, 1, d),
        st(fk, (b, t, FOX_HEADS, HEAD_DIM)), st(fv, (b, t, FOX_HEADS, HEAD_DIM)),
        st(lf, (b, t, FOX_HEADS)),
        st(dk, (b, t, DIFF_HEADS, 2, HEAD_DIM)), st(dv, (b, t, DIFF_HEADS, 2 * HEAD_DIM)),
        st(mk, (b, m_tok, MEM_HEADS, MEM_HEAD_DIM)), st(mv, (b, m_tok, MEM_HEADS, MEM_HEAD_DIM)),
        st(p_conv, (b, CONV_WIDTH - 1, 2 * dff)),
        st(sk, (ns, 1, FOX_HEADS, HEAD_DIM)), st(sv, (ns, 1, FOX_HEADS, HEAD_DIM)),
        st(slf, (ns, 1, FOX_HEADS)),
        st(sdk, (ns, 1, DIFF_HEADS, 2, HEAD_DIM)), st(sdv, (ns, 1, DIFF_HEADS, 2 * HEAD_DIM)),
        st(s_conv, (ns, CONV_WIDTH - 1, 2 * dff)),
    )
```
